```python
import jax, jax.numpy as jnp
from jax import lax
import numpy as np

D_MODEL = 1024
BATCH = 8
SEQ = 2048
DEPTH = 4

CHUNK = 64
Q_BLOCK = 128
EPS = 1e-6
NEG_INF = -1e30

FOX_HEADS = 8
FOX_HEAD_DIM = 64
FOX_WIDTH = FOX_HEADS * FOX_HEAD_DIM
CONV_WIDTH = D_MODEL - FOX_WIDTH
CONV_TAPS = 3
FOX_GATE_BIAS_MEAN = 2.0
AB_IN_COLS = 3 * FOX_WIDTH + FOX_HEADS + 3 * CONV_WIDTH

HGRN_HEADS = 8
HGRN_HEAD_DIM = D_MODEL // HGRN_HEADS
HGRN_WIDTH = HGRN_HEADS * HGRN_HEAD_DIM
C_IN_COLS = 4 * HGRN_WIDTH

FFN_HIDDEN = -(-8 * D_MODEL // (3 * 256)) * 256

N_AB_LAYERS = (DEPTH + 1) // 2
N_C_LAYERS = DEPTH // 2

kernel_name = "hybrid_fox_shortconv_hgrn2_trunk"


def rms_norm(x, g):
    xf = x.astype(jnp.float32)
    y = xf * lax.rsqrt(jnp.mean(xf * xf, axis=-1, keepdims=True) + EPS)
    return (y * g.astype(jnp.float32)).astype(x.dtype)


def forgetting_attention(q, k, v, f_logit):
    S = q.shape[1]
    scale = FOX_HEAD_DIM ** -0.5
    c = jnp.cumsum(jax.nn.log_sigmoid(f_logit.astype(jnp.float32)), axis=1)
    c = jnp.transpose(c, (0, 2, 1))
    outs = []
    for blk in range(S // Q_BLOCK):
        q0 = blk * Q_BLOCK
        q1 = q0 + Q_BLOCK
        s = jnp.einsum('bqhd,bkhd->bhqk', q[:, q0:q1], k[:, :q1],
                       preferred_element_type=jnp.float32) * scale
        s = s + c[:, :, q0:q1, None] - c[:, :, None, :q1]
        mask = (q0 + jnp.arange(Q_BLOCK))[:, None] >= jnp.arange(q1)[None, :]
        s = jnp.where(mask, s, NEG_INF)
        p = jax.nn.softmax(s, axis=-1)
        outs.append(jnp.einsum('bhqk,bkhd->bqhd', p.astype(v.dtype), v[:, :q1]))
    return jnp.concatenate(outs, axis=1)


def short_gated_conv(u_b, u_c, u_x, conv_w):
    S = u_x.shape[1]
    z = u_c * u_x
    zp = jnp.pad(z, ((0, 0), (CONV_TAPS - 1, 0), (0, 0)))
    y = conv_w[0] * zp[:, 0:S]
    for j in range(1, CONV_TAPS):
        y = y + conv_w[j] * zp[:, j:j + S]
    return u_b * y


def hgrn2_chunkwise(q, k, v, log_f):
    Bn, S, H, Dk = q.shape
    Dv = v.shape[-1]
    n = S // CHUNK

    def to_chunks(a):
        return jnp.transpose(a.reshape(Bn, n, CHUNK, H, a.shape[-1]), (1, 0, 3, 2, 4))

    qc, kc, vc, gc = to_chunks(q), to_chunks(k), to_chunks(v), to_chunks(log_f)
    causal = jnp.tril(jnp.ones((CHUNK, CHUNK), dtype=bool))

    def step(state, inp):
        qi, ki, vi, gi = inp
        b = jnp.cumsum(gi, axis=2)
        o_inter = jnp.einsum('bhtd,bhde->bhte', qi * jnp.exp(b), state)
        rel = jnp.where(causal[:, :, None], b[:, :, :, None, :] - b[:, :, None, :, :], -jnp.inf)
        a = jnp.einsum('bhtd,bhsd,bhtsd->bhts', qi, ki, jnp.exp(rel))
        o = o_inter + jnp.einsum('bhts,bhse->bhte', a, vi)
        b_last = b[:, :, -1]
        k_dec = ki * jnp.exp(b_last[:, :, None, :] - b)
        state = jnp.exp(b_last)[..., None] * state + jnp.einsum('bhsd,bhse->bhde', k_dec, vi)
        return state, o

    s0 = jnp.zeros((Bn, H, Dk, Dv), jnp.float32)
    _, o = lax.scan(step, s0, (qc, kc, vc, gc))
    return jnp.transpose(o, (1, 0, 3, 2, 4)).reshape(Bn, S, H, Dv)


def fox_conv_layer(h, w_in, f_bias, conv_w, w_out):
    Bn, S, _ = h.shape
    proj = h @ w_in
    cuts = [FOX_WIDTH, 2 * FOX_WIDTH, 3 * FOX_WIDTH, 3 * FOX_WIDTH + FOX_HEADS,
            3 * FOX_WIDTH + FOX_HEADS + CONV_WIDTH, 3 * FOX_WIDTH + FOX_HEADS + 2 * CONV_WIDTH]
    q, k, v, f_logit, u_b, u_c, u_x = jnp.split(proj, cuts, axis=-1)
    heads = lambda a: a.reshape(Bn, S, FOX_HEADS, FOX_HEAD_DIM)
    a_out = forgetting_attention(heads(q), heads(k), heads(v), f_logit + f_bias)
    a_out = a_out.reshape(Bn, S, FOX_WIDTH).astype(h.dtype)
    b_out = short_gated_conv(u_b, u_c, u_x, conv_w).astype(h.dtype)
    return (jnp.concatenate([a_out, b_out], axis=-1) @ w_out).astype(h.dtype)


def hgrn2_layer(h, w_in, lower_bound, head_norm, w_out):
    Bn, S, _ = h.shape
    proj = (h @ w_in).astype(jnp.float32)
    q, f_logit, i_in, g_out = jnp.split(proj, 4, axis=-1)
    f = lower_bound + (1.0 - lower_bound) * jax.nn.sigmoid(f_logit)
    log_f = jnp.log(f)
    k = 1.0 - f
    heads = lambda a: a.reshape(Bn, S, HGRN_HEADS, HGRN_HEAD_DIM)
    o = hgrn2_chunkwise(heads(q), heads(k), heads(i_in), heads(log_f))
    o = o * lax.rsqrt(jnp.mean(o * o, axis=-1, keepdims=True) + EPS)
    o = o * head_norm.astype(jnp.float32).reshape(HGRN_HEADS, HGRN_HEAD_DIM)
    o = o.reshape(Bn, S, HGRN_WIDTH) * jax.nn.silu(g_out)
    return (o.astype(h.dtype) @ w_out).astype(h.dtype)


def swiglu(h, w_in, w_out):
    gate, up = jnp.split(h @ w_in, 2, axis=-1)
    return ((jax.nn.silu(gate) * up) @ w_out).astype(h.dtype)


def setup_inputs(seed: int = 0) -> dict:
    key = jax.random.key(seed)
    ks = jax.random.split(key, 14)
    f32 = jnp.float32

    def w(k, shape, fan_in):
        return jax.random.normal(k, shape, f32) * (fan_in ** -0.5)

    def gain(k, shape):
        return 1.0 + 0.02 * jax.random.normal(k, shape, f32)

    return {
        "x": jax.random.normal(ks[0], (BATCH, SEQ, D_MODEL), f32),
        "norm_mix": gain(ks[1], (DEPTH, D_MODEL)),
        "norm_ffn": gain(ks[2], (DEPTH, D_MODEL)),
        "final_norm": gain(ks[3], (D_MODEL,)),
        "ab_w_in": w(ks[4], (N_AB_LAYERS, D_MODEL, AB_IN_COLS), D_MODEL),
        "fox_f_bias": FOX_GATE_BIAS_MEAN + 0.5 * jax.random.normal(ks[5], (N_AB_LAYERS, FOX_HEADS), f32),
        "conv_w": w(ks[6], (N_AB_LAYERS, CONV_TAPS, CONV_WIDTH), CONV_TAPS),
        "ab_w_out": w(ks[7], (N_AB_LAYERS, D_MODEL, D_MODEL), D_MODEL),
        "c_w_in": w(ks[8], (N_C_LAYERS, D_MODEL, C_IN_COLS), D_MODEL),
        "c_lower_bounds": 0.5 * jax.random.normal(ks[9], (N_C_LAYERS, HGRN_WIDTH), f32),
        "c_head_norm": gain(ks[10], (N_C_LAYERS, HGRN_WIDTH)),
        "c_w_out": w(ks[11], (N_C_LAYERS, HGRN_WIDTH, D_MODEL), HGRN_WIDTH),
        "ffn_w_in": w(ks[12], (DEPTH, D_MODEL, 2 * FFN_HIDDEN), D_MODEL),
        "ffn_w_out": w(ks[13], (DEPTH, FFN_HIDDEN, D_MODEL), FFN_HIDDEN),
    }


def reference(x, norm_mix, norm_ffn, final_norm, ab_w_in, fox_f_bias, conv_w, ab_w_out,
              c_w_in, c_lower_bounds, c_head_norm, c_w_out, ffn_w_in, ffn_w_out):
    lb = jax.nn.softmax(c_lower_bounds.astype(jnp.float32), axis=0)
    lb = jnp.cumsum(lb, axis=0) - lb[0]
    for layer in range(DEPTH):
        h = rms_norm(x, norm_mix[layer])
        j = layer // 2
        if layer % 2 == 0:
            mix = fox_conv_layer(h, ab_w_in[j], fox_f_bias[j], conv_w[j], ab_w_out[j])
        else:
            mix = hgrn2_layer(h, c_w_in[j], lb[j], c_head_norm[j], c_w_out[j])
        x = x + mix.astype(x.dtype)
        h = rms_norm(x, norm_ffn[layer])
        x = x + swiglu(h, ffn_w_in[layer], ffn_w_out[layer]).astype(x.dtype)
    return rms_norm(x, final_norm)
```

```python
import functools

import jax
import jax.numpy as jnp
from jax import lax
from jax.experimental import pallas as pl
from jax.experimental.pallas import tpu as pltpu

D_MODEL = 1024
EPS = 1e-6
NEG_BIG = -1e30

FOX_HEADS = 8
FOX_HEAD_DIM = 64
FOX_WIDTH = FOX_HEADS * FOX_HEAD_DIM
CONV_WIDTH = D_MODEL - FOX_WIDTH
CONV_TAPS = 3
HGRN_HEADS = 8
HGRN_HEAD_DIM = D_MODEL // HGRN_HEADS
HGRN_CHUNK = 64
HGRN_SUB = 16

LANES = 128
SUBLANES = 8
VMEM_LIMIT = 56 * 1024 * 1024

BF16 = jnp.bfloat16
F32 = jnp.float32

NT_DIMS = (((1,), (1,)), ((), ()))
TN_DIMS = (((0,), (0,)), ((), ()))


def _resident(shape):
    zeros = (0,) * len(shape)
    return pl.BlockSpec(shape, lambda *_: zeros, pipeline_mode=pl.Buffered(1))


def _params(*sem):
    return pltpu.CompilerParams(dimension_semantics=sem, vmem_limit_bytes=VMEM_LIMIT)


def _rms_norm(x, g):
    return x * lax.rsqrt(jnp.mean(x * x, axis=-1, keepdims=True) + EPS) * g


def _sigmoid(x):
    return 1.0 / (1.0 + jnp.exp(-x))


def _ab_in_kernel(x_ref, g_ref, wqkv_ref, wf_ref, wconv_ref, cw_ref,
                  qkv_ref, f_ref, b_ref, zbuf, *, tm, seq):
    i = pl.program_id(0)
    h = _rms_norm(x_ref[...], g_ref[...]).astype(BF16)
    qkv_ref[...] = jnp.dot(h, wqkv_ref[...], preferred_element_type=F32).astype(BF16)
    f_ref[...] = jnp.dot(h, wf_ref[...], preferred_element_type=F32)
    u = jnp.dot(h, wconv_ref[...], preferred_element_type=F32)
    u_b = u[:, :CONV_WIDTH]
    z = u[:, CONV_WIDTH:2 * CONV_WIDTH] * u[:, 2 * CONV_WIDTH:]

    @pl.when((i * tm) % seq == 0)
    def _():
        zbuf[0:SUBLANES, :] = jnp.zeros((SUBLANES, CONV_WIDTH), F32)

    zbuf[SUBLANES:SUBLANES + tm, :] = z
    cw = cw_ref[...]
    y = (cw[0:1, :] * zbuf[SUBLANES - 2:SUBLANES - 2 + tm, :]
         + cw[1:2, :] * zbuf[SUBLANES - 1:SUBLANES - 1 + tm, :]
         + cw[2:3, :] * z)
    b_ref[...] = (u_b * y).astype(BF16)
    zbuf[0:SUBLANES, :] = zbuf[tm:tm + SUBLANES, :]


def _ab_in(x2d, g, wqkv, wf, wconv, cw, *, seq, tm=512):
    t = x2d.shape[0]
    kern = functools.partial(_ab_in_kernel, tm=tm, seq=seq)
    return pl.pallas_call(
        kern,
        grid=(t // tm,),
        in_specs=[
            pl.BlockSpec((tm, D_MODEL), lambda i: (i, 0)),
            _resident((1, D_MODEL)),
            _resident(wqkv.shape),
            _resident(wf.shape),
            _resident(wconv.shape),
            _resident(cw.shape),
        ],
        out_specs=[
            pl.BlockSpec((tm, 3 * FOX_WIDTH), lambda i: (i, 0)),
            pl.BlockSpec((tm, LANES), lambda i: (i, 0)),
            pl.BlockSpec((tm, CONV_WIDTH), lambda i: (i, 0)),
        ],
        out_shape=[
            jax.ShapeDtypeStruct((t, 3 * FOX_WIDTH), BF16),
            jax.ShapeDtypeStruct((t, LANES), F32),
            jax.ShapeDtypeStruct((t, CONV_WIDTH), BF16),
        ],
        scratch_shapes=[pltpu.VMEM((tm + SUBLANES, CONV_WIDTH), F32)],
        compiler_params=_params("arbitrary"),
        name="ab_in",
    )(x2d, g, wqkv, wf, wconv, cw)


def _fox_cumsum_kernel(f_ref, bias_ref, c_ref, *, seq):
    x = f_ref[...] + bias_ref[...]
    ls = jnp.minimum(x, 0.0) - jnp.log(1.0 + jnp.exp(-jnp.abs(x)))
    rows = ls.shape[0]
    lane = lax.broadcasted_iota(jnp.int32, (rows, LANES), 1)
    carry = jnp.zeros((rows, 1), F32)
    for j in range(seq // LANES):
        blk = ls[:, j * LANES:(j + 1) * LANES]
        shift = 1
        while shift < LANES:
            blk = blk + jnp.where(lane >= shift, pltpu.roll(blk, shift, 1), 0.0)
            shift *= 2
        blk = blk + carry
        c_ref[:, j * LANES:(j + 1) * LANES] = blk
        carry = blk[:, LANES - 1:LANES]


def _fox_cumsum(f_rows, bias_rows):
    rows, seq = f_rows.shape
    return pl.pallas_call(
        functools.partial(_fox_cumsum_kernel, seq=seq),
        out_shape=jax.ShapeDtypeStruct((rows, seq), F32),
        name="fox_cumsum",
    )(f_rows, bias_rows)


def _fox_attn_kernel(q_ref, k_ref, v_ref, ccol_ref, crow_ref, o_ref, *, tq):
    qi = pl.program_id(2)
    lane = lax.broadcasted_iota(jnp.int32, (1, LANES), 1)
    scale = FOX_HEAD_DIM ** -0.5
    q = q_ref[0]
    row = lax.broadcasted_iota(jnp.int32, (tq, tq), 0)
    col = lax.broadcasted_iota(jnp.int32, (tq, tq), 1)
    causal = row >= col
    outs = []
    for hh in range(2):
        in_head = (lane >= hh * FOX_HEAD_DIM) & (lane < (hh + 1) * FOX_HEAD_DIM)
        qh = jnp.where(in_head, q, jnp.zeros_like(q)) * jnp.asarray(scale, q.dtype)
        ct = ccol_ref[0, hh]

        def block(j, carry, masked):
            m, l, acc = carry
            start = pl.multiple_of(j * tq, tq)
            k = k_ref[0, pl.ds(start, tq), :]
            v = v_ref[0, pl.ds(start, tq), :]
            s = lax.dot_general(qh, k, NT_DIMS, preferred_element_type=F32)
            s = s + ct - crow_ref[0, hh, pl.ds(j, 1), :]
            if masked:
                s = jnp.where(causal, s, NEG_BIG)
            m_new = jnp.maximum(m, jnp.max(s, axis=-1, keepdims=True))
            alpha = jnp.exp(m - m_new)
            p = jnp.exp(s - m_new)
            l = alpha * l + jnp.sum(p, axis=-1, keepdims=True)
            acc = alpha * acc + jnp.dot(p.astype(BF16), v, preferred_element_type=F32)
            return m_new, l, acc

        init = (jnp.full((tq, 1), NEG_BIG, F32), jnp.zeros((tq, 1), F32),
                jnp.zeros((tq, LANES), F32))
        carry = lax.fori_loop(0, qi, functools.partial(block, masked=False), init)
        _, l, acc = block(qi, carry, True)
        outs.append(acc / l)
    o_ref[0] = jnp.where(lane < FOX_HEAD_DIM, outs[0], outs[1]).astype(o_ref.dtype)


def _fox_attn(qkv, c_col, c_row, *, tq=256):
    bsz, seq, _ = qkv.shape
    n_pairs = FOX_WIDTH // LANES
    nblk = seq // tq
    return pl.pallas_call(
        functools.partial(_fox_attn_kernel, tq=tq),
        grid=(bsz, n_pairs, nblk),
        in_specs=[
            pl.BlockSpec((1, tq, LANES), lambda b, p, i: (b, i, p)),
            pl.BlockSpec((1, seq, LANES), lambda b, p, i: (b, 0, n_pairs + p)),
            pl.BlockSpec((1, seq, LANES), lambda b, p, i: (b, 0, 2 * n_pairs + p)),
            pl.BlockSpec((1, 2, tq, 1), lambda b, p, i: (b, p, i, 0)),
            pl.BlockSpec((1, 2, nblk, tq), lambda b, p, i: (b, p, 0, 0)),
        ],
        out_specs=pl.BlockSpec((1, tq, LANES), lambda b, p, i: (b, i, p)),
        out_shape=jax.ShapeDtypeStruct((bsz, seq, FOX_WIDTH), BF16),
        compiler_params=_params("parallel", "parallel", "arbitrary"),
        name="fox_attn",
    )(qkv, qkv, qkv, c_col, c_row)


def _post_kernel(*refs, n_parts, final):
    parts = refs[:n_parts]
    x_ref, wo_ref, g_ref, w1_ref, w2_ref, gf_ref, o_ref = refs[n_parts:]
    hidden = w2_ref.shape[0]
    mix = None
    off = 0
    for p_ref in parts:
        width = p_ref.shape[1]
        d = jnp.dot(p_ref[...], wo_ref[off:off + width, :], preferred_element_type=F32)
        mix = d if mix is None else mix + d
        off += width
    x1 = x_ref[...] + mix
    h = _rms_norm(x1, g_ref[...]).astype(BF16)
    gu = jnp.dot(h, w1_ref[...], preferred_element_type=F32)
    gate = gu[:, :hidden]
    act = (gate * _sigmoid(gate) * gu[:, hidden:]).astype(BF16)
    x2 = x1 + jnp.dot(act, w2_ref[...], preferred_element_type=F32)
    if final:
        x2 = _rms_norm(x2, gf_ref[...])
    o_ref[...] = x2


def _post(parts, x2d, wo, g, w1, w2, gf, *, final, tm=256):
    t = x2d.shape[0]
    kern = functools.partial(_post_kernel, n_parts=len(parts), final=final)
    in_specs = [pl.BlockSpec((tm, p.shape[1]), lambda i: (i, 0)) for p in parts]
    in_specs += [
        pl.BlockSpec((tm, D_MODEL), lambda i: (i, 0)),
        _resident(wo.shape),
        _resident((1, D_MODEL)),
        _resident(w1.shape),
        _resident(w2.shape),
        _resident((1, D_MODEL)),
    ]
    return pl.pallas_call(
        kern,
        grid=(t // tm,),
        in_specs=in_specs,
        out_specs=pl.BlockSpec((tm, D_MODEL), lambda i: (i, 0)),
        out_shape=jax.ShapeDtypeStruct((t, D_MODEL), F32),
        compiler_params=_params("parallel"),
        name="post",
    )(*parts, x2d, wo, g, w1, w2, gf)


def _c_in_kernel(x_ref, g_ref, w_ref, o_ref, f_ref):
    h = _rms_norm(x_ref[...], g_ref[...]).astype(BF16)
    pr = jnp.dot(h, w_ref[...], preferred_element_type=F32)
    o_ref[...] = pr.astype(BF16)
    f_ref[...] = pr[:, D_MODEL:2 * D_MODEL]


def _c_in(x2d, g, w, *, tm=512):
    t = x2d.shape[0]
    return pl.pallas_call(
        _c_in_kernel,
        grid=(t // tm,),
        in_specs=[
            pl.BlockSpec((tm, D_MODEL), lambda i: (i, 0)),
            _resident((1, D_MODEL)),
            _resident(w.shape),
        ],
        out_specs=[
            pl.BlockSpec((tm, 4 * D_MODEL), lambda i: (i, 0)),
            pl.BlockSpec((tm, D_MODEL), lambda i: (i, 0)),
        ],
        out_shape=[
            jax.ShapeDtypeStruct((t, 4 * D_MODEL), BF16),
            jax.ShapeDtypeStruct((t, D_MODEL), F32),
        ],
        compiler_params=_params("parallel"),
        name="c_in",
    )(x2d, g, w)


def _sublane_cumsum(x, row):
    shift = 1
    while shift < x.shape[0]:
        x = x + jnp.where(row >= shift, pltpu.roll(x, shift, 0), 0.0)
        shift *= 2
    return x


def _hgrn_kernel(q_ref, f_ref, i_ref, g_ref, lbraw_ref, hn_ref, o_ref, state, *, layer_idx, blk):
    @pl.when(pl.program_id(2) == 0)
    def _():
        state[...] = jnp.zeros_like(state)

    raw = lbraw_ref[...]
    e = jnp.exp(raw - jnp.max(raw, axis=0, keepdims=True))
    sm = e / jnp.sum(e, axis=0, keepdims=True)
    lb = jnp.sum(sm[0:layer_idx + 1, :], axis=0, keepdims=True) - sm[0:1, :]
    hn = hn_ref[...]

    c = HGRN_CHUNK
    row = lax.broadcasted_iota(jnp.int32, (c, HGRN_HEAD_DIM), 0)
    sub = row // HGRN_SUB
    n_sub = c // HGRN_SUB

    for ci in range(blk // c):
        sl = slice(ci * c, (ci + 1) * c)
        q = q_ref[0, sl, :].astype(F32)
        v = i_ref[0, sl, :]
        v32 = v.astype(F32)
        f = lb + (1.0 - lb) * _sigmoid(f_ref[0, sl, :])
        k = 1.0 - f
        b = _sublane_cumsum(jnp.log(f), row)
        st = state[...]

        o = lax.dot_general((q * jnp.exp(b)).astype(BF16), st.astype(BF16), NT_DIMS,
                            preferred_element_type=F32)

        lhs, rhs = [], []
        for si in range(1, n_sub):
            bref = b[si * HGRN_SUB - 1:si * HGRN_SUB, :]
            lhs.append(q * jnp.exp(jnp.where(sub == si, b - bref, NEG_BIG)))
            rhs.append(k * jnp.exp(jnp.where(sub < si, bref - b, NEG_BIG)))
        a_off = lax.dot_general(jnp.concatenate(lhs, axis=1).astype(BF16),
                                jnp.concatenate(rhs, axis=1).astype(BF16), NT_DIMS,
                                preferred_element_type=F32)
        o = o + jnp.dot(a_off.astype(BF16), v, preferred_element_type=F32)

        for d in range(HGRN_SUB):
            valid = (row % HGRN_SUB) >= d
            if d == 0:
                k_d, b_d, v_d = k, b, v32
            else:
                k_d, b_d, v_d = pltpu.roll(k, d, 0), pltpu.roll(b, d, 0), pltpu.roll(v32, d, 0)
            w = q * k_d * jnp.exp(jnp.where(valid, b - b_d, NEG_BIG))
            o = o + jnp.sum(w, axis=-1, keepdims=True) * v_d

        b_last = b[c - 1:c, :]
        k_dec = (k * jnp.exp(b_last - b)).astype(BF16)
        state[...] = st * jnp.exp(b_last) + lax.dot_general(v, k_dec, TN_DIMS,
                                                            preferred_element_type=F32)

        o = o * lax.rsqrt(jnp.mean(o * o, axis=-1, keepdims=True) + EPS) * hn
        g = g_ref[0, sl, :].astype(F32)
        o_ref[0, sl, :] = (o * (g * _sigmoid(g))).astype(o_ref.dtype)


def _hgrn(proj, f_logit, lb_raw, head_norm, *, layer_idx, blk=512):
    bsz, seq, _ = f_logit.shape
    hd = HGRN_HEAD_DIM
    nh = HGRN_HEADS
    kern = functools.partial(_hgrn_kernel, layer_idx=layer_idx, blk=blk)
    return pl.pallas_call(
        kern,
        grid=(bsz, nh, seq // blk),
        in_specs=[
            pl.BlockSpec((1, blk, hd), lambda b, h, l: (b, l, h)),
            pl.BlockSpec((1, blk, hd), lambda b, h, l: (b, l, h)),
            pl.BlockSpec((1, blk, hd), lambda b, h, l: (b, l, 2 * nh + h)),
            pl.BlockSpec((1, blk, hd), lambda b, h, l: (b, l, 3 * nh + h)),
            pl.BlockSpec((lb_raw.shape[0], hd), lambda b, h, l: (0, h)),
            pl.BlockSpec((1, hd), lambda b, h, l: (0, h)),
        ],
        out_specs=pl.BlockSpec((1, blk, hd), lambda b, h, l: (b, l, h)),
        out_shape=jax.ShapeDtypeStruct((bsz, seq, D_MODEL), BF16),
        scratch_shapes=[pltpu.VMEM((hd, hd), F32)],
        compiler_params=_params("parallel", "parallel", "arbitrary"),
        name="hgrn",
    )(proj, f_logit, proj, proj, lb_raw, head_norm)


def kernel(x, norm_mix, norm_ffn, final_norm, ab_w_in, fox_f_bias, conv_w, ab_w_out,
           c_w_in, c_lower_bounds, c_head_norm, c_w_out, ffn_w_in, ffn_w_out):
    bsz, seq, d = x.shape
    depth = norm_mix.shape[0]
    t = bsz * seq
    x2d = x.reshape(t, d)
    gf = final_norm.reshape(1, d)
    q_end = 3 * FOX_WIDTH
    f_end = q_end + FOX_HEADS

    for layer in range(depth):
        j = layer // 2
        g_mix = norm_mix[layer].reshape(1, d)
        if layer % 2 == 0:
            w_in = ab_w_in[j]
            wqkv = w_in[:, :q_end].astype(BF16)
            wf = jnp.pad(w_in[:, q_end:f_end], ((0, 0), (0, LANES - FOX_HEADS))).astype(BF16)
            wconv = w_in[:, f_end:].astype(BF16)
            qkv, f_pad, b_out = _ab_in(x2d, g_mix, wqkv, wf, wconv, conv_w[j], seq=seq)
            f_rows = f_pad[:, :FOX_HEADS].reshape(bsz, seq, FOX_HEADS)
            f_rows = jnp.transpose(f_rows, (0, 2, 1)).reshape(bsz * FOX_HEADS, seq)
            bias_rows = jnp.tile(fox_f_bias[j], bsz).reshape(bsz * FOX_HEADS, 1)
            c = _fox_cumsum(f_rows, bias_rows)
            tq = 256
            c_col = c.reshape(bsz, FOX_HEADS, seq, 1)
            c_row = c.reshape(bsz, FOX_HEADS, seq // tq, tq)
            a_out = _fox_attn(qkv.reshape(bsz, seq, q_end), c_col, c_row, tq=tq)
            parts = [a_out.reshape(t, FOX_WIDTH), b_out]
            w_out = ab_w_out[j]
        else:
            proj, f_logit = _c_in(x2d, g_mix, c_w_in[j].astype(BF16))
            o = _hgrn(proj.reshape(bsz, seq, 4 * d), f_logit.reshape(bsz, seq, d),
                      c_lower_bounds, c_head_norm[j].reshape(1, d), layer_idx=j)
            parts = [o.reshape(t, d)]
            w_out = c_w_out[j]
        x2d = _post(parts, x2d, w_out.astype(BF16), norm_ffn[layer].reshape(1, d),
                    ffn_w_in[layer].astype(BF16), ffn_w_out[layer].astype(BF16), gf,
                    final=(layer == depth - 1))
    return x2d.reshape(bsz, seq, d)
```

```python
import functools

import jax
import jax.numpy as jnp
from jax import lax
from jax.experimental import pallas as pl
from jax.experimental.pallas import tpu as pltpu

D_MODEL = 1024
EPS = 1e-6
NEG_BIG = -1e30

FOX_HEADS = 8
FOX_HEAD_DIM = 64
FOX_WIDTH = FOX_HEADS * FOX_HEAD_DIM
CONV_WIDTH = D_MODEL - FOX_WIDTH
CONV_TAPS = 3
LOG2E = 1.4426950408889634
FOX_Q_SCALE = LOG2E * FOX_HEAD_DIM ** -0.5
HGRN_HEADS = 8
HGRN_HEAD_DIM = D_MODEL // HGRN_HEADS
HGRN_CHUNK = 64
HGRN_SUB = 16

LANES = 128
SUBLANES = 8
VMEM_LIMIT = 56 * 1024 * 1024

BF16 = jnp.bfloat16
F32 = jnp.float32

NT_DIMS = (((1,), (1,)), ((), ()))
TN_DIMS = (((0,), (0,)), ((), ()))


def _resident(shape):
    zeros = (0,) * len(shape)
    return pl.BlockSpec(shape, lambda *_: zeros, pipeline_mode=pl.Buffered(1))


def _params(*sem):
    return pltpu.CompilerParams(dimension_semantics=sem, vmem_limit_bytes=VMEM_LIMIT)


def _rms_norm(x, g):
    return x * lax.rsqrt(jnp.mean(x * x, axis=-1, keepdims=True) + EPS) * g


def _sigmoid(x):
    return 1.0 / (1.0 + jnp.exp(-x))


def _ab_in_kernel(x_ref, g_ref, wqkv_ref, wf_ref, wconv_ref, cw_ref,
                  qkv_ref, f_ref, b_ref, zbuf, *, tm, seq):
    i = pl.program_id(0)
    h = _rms_norm(x_ref[...], g_ref[...]).astype(BF16)
    qkv = jnp.dot(h, wqkv_ref[...], preferred_element_type=F32)
    qkv_ref[:, :FOX_WIDTH] = (qkv[:, :FOX_WIDTH] * FOX_Q_SCALE).astype(BF16)
    qkv_ref[:, FOX_WIDTH:] = qkv[:, FOX_WIDTH:].astype(BF16)
    f_ref[...] = jnp.dot(h, wf_ref[...], preferred_element_type=F32)
    u = jnp.dot(h, wconv_ref[...], preferred_element_type=F32)
    u_b = u[:, :CONV_WIDTH]
    z = u[:, CONV_WIDTH:2 * CONV_WIDTH] * u[:, 2 * CONV_WIDTH:]

    @pl.when((i * tm) % seq == 0)
    def _():
        zbuf[0:SUBLANES, :] = jnp.zeros((SUBLANES, CONV_WIDTH), F32)

    zbuf[SUBLANES:SUBLANES + tm, :] = z
    cw = cw_ref[...]
    y = (cw[0:1, :] * zbuf[SUBLANES - 2:SUBLANES - 2 + tm, :]
         + cw[1:2, :] * zbuf[SUBLANES - 1:SUBLANES - 1 + tm, :]
         + cw[2:3, :] * z)
    b_ref[...] = (u_b * y).astype(BF16)
    zbuf[0:SUBLANES, :] = zbuf[tm:tm + SUBLANES, :]


def _ab_in(x2d, g, wqkv, wf, wconv, cw, *, seq, tm=512):
    t = x2d.shape[0]
    kern = functools.partial(_ab_in_kernel, tm=tm, seq=seq)
    return pl.pallas_call(
        kern,
        grid=(t // tm,),
        in_specs=[
            pl.BlockSpec((tm, D_MODEL), lambda i: (i, 0)),
            _resident((1, D_MODEL)),
            _resident(wqkv.shape),
            _resident(wf.shape),
            _resident(wconv.shape),
            _resident(cw.shape),
        ],
        out_specs=[
            pl.BlockSpec((tm, 3 * FOX_WIDTH), lambda i: (i, 0)),
            pl.BlockSpec((tm, LANES), lambda i: (i, 0)),
            pl.BlockSpec((tm, CONV_WIDTH), lambda i: (i, 0)),
        ],
        out_shape=[
            jax.ShapeDtypeStruct((t, 3 * FOX_WIDTH), BF16),
            jax.ShapeDtypeStruct((t, LANES), F32),
            jax.ShapeDtypeStruct((t, CONV_WIDTH), BF16),
        ],
        scratch_shapes=[pltpu.VMEM((tm + SUBLANES, CONV_WIDTH), F32)],
        compiler_params=_params("arbitrary"),
        name="ab_in",
    )(x2d, g, wqkv, wf, wconv, cw)


def _fox_cumsum_kernel(f_ref, bias_ref, c_ref, *, seq):
    x = f_ref[...] + bias_ref[...]
    ls = jnp.minimum(x, 0.0) - jnp.log(1.0 + jnp.exp(-jnp.abs(x)))
    rows = ls.shape[0]
    lane = lax.broadcasted_iota(jnp.int32, (rows, LANES), 1)
    carry = jnp.zeros((rows, 1), F32)
    for j in range(seq // LANES):
        blk = ls[:, j * LANES:(j + 1) * LANES]
        shift = 1
        while shift < LANES:
            blk = blk + jnp.where(lane >= shift, pltpu.roll(blk, shift, 1), 0.0)
            shift *= 2
        blk = blk + carry
        c_ref[:, j * LANES:(j + 1) * LANES] = blk * LOG2E
        carry = blk[:, LANES - 1:LANES]


def _fox_cumsum(f_rows, bias_rows):
    rows, seq = f_rows.shape
    return pl.pallas_call(
        functools.partial(_fox_cumsum_kernel, seq=seq),
        out_shape=jax.ShapeDtypeStruct((rows, seq), F32),
        name="fox_cumsum",
    )(f_rows, bias_rows)


def _fox_attn_kernel(q_ref, k_ref, v_ref, ccol_ref, crow_ref, o_ref, *, tq):
    qi = pl.program_id(2)
    lane = lax.broadcasted_iota(jnp.int32, (1, LANES), 1)
    q = q_ref[0]
    row = lax.broadcasted_iota(jnp.int32, (tq, tq), 0)
    col = lax.broadcasted_iota(jnp.int32, (tq, tq), 1)
    causal = row >= col
    qh = [jnp.where((lane >= hh * FOX_HEAD_DIM) & (lane < (hh + 1) * FOX_HEAD_DIM),
                    q, jnp.zeros_like(q)) for hh in range(2)]
    ct = [ccol_ref[0, hh] for hh in range(2)]

    def attend(n_kv):
        s = [[], []]
        for j in range(n_kv):
            k = k_ref[0, j * tq:(j + 1) * tq, :]
            for hh in range(2):
                sh = lax.dot_general(qh[hh], k, NT_DIMS, preferred_element_type=F32)
                sh = sh - crow_ref[0, hh, j:j + 1, :]
                s[hh].append(jnp.where(causal, sh, NEG_BIG) if j == n_kv - 1 else sh)
        shift = []
        for hh in range(2):
            mx = None
            for sj in s[hh]:
                for cb in range(tq // LANES):
                    chunk = sj[:, cb * LANES:(cb + 1) * LANES]
                    mx = chunk if mx is None else jnp.maximum(mx, chunk)
            m = jnp.max(mx, axis=-1, keepdims=True) + ct[hh]
            shift.append(m - ct[hh])
        acc = [None, None]
        for j in range(n_kv):
            v = v_ref[0, j * tq:(j + 1) * tq, :]
            v_aug = jnp.concatenate([v, jnp.ones_like(v)], axis=1)
            for hh in range(2):
                p = jnp.exp2(s[hh][j] - shift[hh]).astype(BF16)
                d = jnp.dot(p, v_aug, preferred_element_type=F32)
                acc[hh] = d if acc[hh] is None else acc[hh] + d
        out = [acc[hh][:, :LANES] / acc[hh][:, LANES:] for hh in range(2)]
        o_ref[0] = jnp.where(lane < FOX_HEAD_DIM, out[0], out[1]).astype(o_ref.dtype)

    for c in range(k_ref.shape[1] // tq):
        pl.when(qi == c)(functools.partial(attend, c + 1))


def _fox_attn(qkv, c_col, c_row, *, tq=256):
    bsz, seq, _ = qkv.shape
    n_pairs = FOX_WIDTH // LANES
    nblk = seq // tq
    return pl.pallas_call(
        functools.partial(_fox_attn_kernel, tq=tq),
        grid=(bsz, n_pairs, nblk),
        in_specs=[
            pl.BlockSpec((1, tq, LANES), lambda b, p, i: (b, i, p)),
            pl.BlockSpec((1, seq, LANES), lambda b, p, i: (b, 0, n_pairs + p)),
            pl.BlockSpec((1, seq, LANES), lambda b, p, i: (b, 0, 2 * n_pairs + p)),
            pl.BlockSpec((1, 2, tq, 1), lambda b, p, i: (b, p, i, 0)),
            pl.BlockSpec((1, 2, nblk, tq), lambda b, p, i: (b, p, 0, 0)),
        ],
        out_specs=pl.BlockSpec((1, tq, LANES), lambda b, p, i: (b, i, p)),
        out_shape=jax.ShapeDtypeStruct((bsz, seq, FOX_WIDTH), BF16),
        compiler_params=_params("parallel", "parallel", "arbitrary"),
        name="fox_attn",
    )(qkv, qkv, qkv, c_col, c_row)


def _post_kernel(*refs, n_parts, final):
    parts = refs[:n_parts]
    x_ref, wo_ref, g_ref, w1_ref, w2_ref, gf_ref, o_ref = refs[n_parts:]
    hidden = w2_ref.shape[0]
    mix = None
    off = 0
    for p_ref in parts:
        width = p_ref.shape[1]
        d = jnp.dot(p_ref[...], wo_ref[off:off + width, :], preferred_element_type=F32)
        mix = d if mix is None else mix + d
        off += width
    x1 = x_ref[...] + mix
    h = _rms_norm(x1, g_ref[...]).astype(BF16)
    gu = jnp.dot(h, w1_ref[...], preferred_element_type=F32)
    gate = gu[:, :hidden]
    act = (gate * _sigmoid(gate) * gu[:, hidden:]).astype(BF16)
    x2 = x1 + jnp.dot(act, w2_ref[...], preferred_element_type=F32)
    if final:
        x2 = _rms_norm(x2, gf_ref[...])
    o_ref[...] = x2


def _post(parts, x2d, wo, g, w1, w2, gf, *, final, tm=256):
    t = x2d.shape[0]
    kern = functools.partial(_post_kernel, n_parts=len(parts), final=final)
    in_specs = [pl.BlockSpec((tm, p.shape[1]), lambda i: (i, 0)) for p in parts]
    in_specs += [
        pl.BlockSpec((tm, D_MODEL), lambda i: (i, 0)),
        _resident(wo.shape),
        _resident((1, D_MODEL)),
        _resident(w1.shape),
        _resident(w2.shape),
        _resident((1, D_MODEL)),
    ]
    return pl.pallas_call(
        kern,
        grid=(t // tm,),
        in_specs=in_specs,
        out_specs=pl.BlockSpec((tm, D_MODEL), lambda i: (i, 0)),
        out_shape=jax.ShapeDtypeStruct((t, D_MODEL), F32),
        compiler_params=_params("parallel"),
        name="post",
    )(*parts, x2d, wo, g, w1, w2, gf)


def _c_in_kernel(x_ref, g_ref, w_ref, o_ref, f_ref):
    h = _rms_norm(x_ref[...], g_ref[...]).astype(BF16)
    pr = jnp.dot(h, w_ref[...], preferred_element_type=F32)
    o_ref[...] = pr.astype(BF16)
    f_ref[...] = pr[:, D_MODEL:2 * D_MODEL]


def _c_in(x2d, g, w, *, tm=512):
    t = x2d.shape[0]
    return pl.pallas_call(
        _c_in_kernel,
        grid=(t // tm,),
        in_specs=[
            pl.BlockSpec((tm, D_MODEL), lambda i: (i, 0)),
            _resident((1, D_MODEL)),
            _resident(w.shape),
        ],
        out_specs=[
            pl.BlockSpec((tm, 4 * D_MODEL), lambda i: (i, 0)),
            pl.BlockSpec((tm, D_MODEL), lambda i: (i, 0)),
        ],
        out_shape=[
            jax.ShapeDtypeStruct((t, 4 * D_MODEL), BF16),
            jax.ShapeDtypeStruct((t, D_MODEL), F32),
        ],
        compiler_params=_params("parallel"),
        name="c_in",
    )(x2d, g, w)


def _sublane_cumsum(x, row):
    shift = 1
    while shift < x.shape[0]:
        x = x + jnp.where(row >= shift, pltpu.roll(x, shift, 0), 0.0)
        shift *= 2
    return x


def _hgrn_kernel(q_ref, f_ref, i_ref, g_ref, lbraw_ref, hn_ref, o_ref, state, *, layer_idx, blk):
    @pl.when(pl.program_id(2) == 0)
    def _():
        state[...] = jnp.zeros_like(state)

    raw = lbraw_ref[...]
    e = jnp.exp(raw - jnp.max(raw, axis=0, keepdims=True))
    sm = e / jnp.sum(e, axis=0, keepdims=True)
    lb = jnp.sum(sm[0:layer_idx + 1, :], axis=0, keepdims=True) - sm[0:1, :]
    hn = hn_ref[...]

    c = HGRN_CHUNK
    row = lax.broadcasted_iota(jnp.int32, (c, HGRN_HEAD_DIM), 0)
    sub = row // HGRN_SUB
    n_sub = c // HGRN_SUB

    for ci in range(blk // c):
        sl = slice(ci * c, (ci + 1) * c)
        q = q_ref[0, sl, :].astype(F32)
        v = i_ref[0, sl, :]
        v32 = v.astype(F32)
        f = lb + (1.0 - lb) * _sigmoid(f_ref[0, sl, :])
        k = 1.0 - f
        b = _sublane_cumsum(jnp.log(f), row)
        st = state[...]

        o = lax.dot_general((q * jnp.exp(b)).astype(BF16), st.astype(BF16), NT_DIMS,
                            preferred_element_type=F32)

        lhs, rhs = [], []
        for si in range(1, n_sub):
            bref = b[si * HGRN_SUB - 1:si * HGRN_SUB, :]
            lhs.append(q * jnp.exp(jnp.where(sub == si, b - bref, NEG_BIG)))
            rhs.append(k * jnp.exp(jnp.where(sub < si, bref - b, NEG_BIG)))
        a_off = lax.dot_general(jnp.concatenate(lhs, axis=1).astype(BF16),
                                jnp.concatenate(rhs, axis=1).astype(BF16), NT_DIMS,
                                preferred_element_type=F32)
        o = o + jnp.dot(a_off.astype(BF16), v, preferred_element_type=F32)

        for d in range(HGRN_SUB):
            valid = (row % HGRN_SUB) >= d
            if d == 0:
                k_d, b_d, v_d = k, b, v32
            else:
                k_d, b_d, v_d = pltpu.roll(k, d, 0), pltpu.roll(b, d, 0), pltpu.roll(v32, d, 0)
            w = q * k_d * jnp.exp(jnp.where(valid, b - b_d, NEG_BIG))
            o = o + jnp.sum(w, axis=-1, keepdims=True) * v_d

        b_last = b[c - 1:c, :]
        k_dec = (k * jnp.exp(b_last - b)).astype(BF16)
        state[...] = st * jnp.exp(b_last) + lax.dot_general(v, k_dec, TN_DIMS,
                                                            preferred_element_type=F32)

        o = o * lax.rsqrt(jnp.mean(o * o, axis=-1, keepdims=True) + EPS) * hn
        g = g_ref[0, sl, :].astype(F32)
        o_ref[0, sl, :] = (o * (g * _sigmoid(g))).astype(o_ref.dtype)


def _hgrn(proj, f_logit, lb_raw, head_norm, *, layer_idx, blk=512):
    bsz, seq, _ = f_logit.shape
    hd = HGRN_HEAD_DIM
    nh = HGRN_HEADS
    kern = functools.partial(_hgrn_kernel, layer_idx=layer_idx, blk=blk)
    return pl.pallas_call(
        kern,
        grid=(bsz, nh, seq // blk),
        in_specs=[
            pl.BlockSpec((1, blk, hd), lambda b, h, l: (b, l, h)),
            pl.BlockSpec((1, blk, hd), lambda b, h, l: (b, l, h)),
            pl.BlockSpec((1, blk, hd), lambda b, h, l: (b, l, 2 * nh + h)),
            pl.BlockSpec((1, blk, hd), lambda b, h, l: (b, l, 3 * nh + h)),
            pl.BlockSpec((lb_raw.shape[0], hd), lambda b, h, l: (0, h)),
            pl.BlockSpec((1, hd), lambda b, h, l: (0, h)),
        ],
        out_specs=pl.BlockSpec((1, blk, hd), lambda b, h, l: (b, l, h)),
        out_shape=jax.ShapeDtypeStruct((bsz, seq, D_MODEL), BF16),
        scratch_shapes=[pltpu.VMEM((hd, hd), F32)],
        compiler_params=_params("parallel", "parallel", "arbitrary"),
        name="hgrn",
    )(proj, f_logit, proj, proj, lb_raw, head_norm)


def kernel(x, norm_mix, norm_ffn, final_norm, ab_w_in, fox_f_bias, conv_w, ab_w_out,
           c_w_in, c_lower_bounds, c_head_norm, c_w_out, ffn_w_in, ffn_w_out):
    bsz, seq, d = x.shape
    depth = norm_mix.shape[0]
    t = bsz * seq
    x2d = x.reshape(t, d)
    gf = final_norm.reshape(1, d)
    q_end = 3 * FOX_WIDTH
    f_end = q_end + FOX_HEADS

    for layer in range(depth):
        j = layer // 2
        g_mix = norm_mix[layer].reshape(1, d)
        if layer % 2 == 0:
            w_in = ab_w_in[j]
            wqkv = w_in[:, :q_end].astype(BF16)
            wf = jnp.pad(w_in[:, q_end:f_end], ((0, 0), (0, LANES - FOX_HEADS))).astype(BF16)
            wconv = w_in[:, f_end:].astype(BF16)
            qkv, f_pad, b_out = _ab_in(x2d, g_mix, wqkv, wf, wconv, conv_w[j], seq=seq)
            f_rows = f_pad[:, :FOX_HEADS].reshape(bsz, seq, FOX_HEADS)
            f_rows = jnp.transpose(f_rows, (0, 2, 1)).reshape(bsz * FOX_HEADS, seq)
            bias_rows = jnp.tile(fox_f_bias[j], bsz).reshape(bsz * FOX_HEADS, 1)
            c = _fox_cumsum(f_rows, bias_rows)
            tq = 256
            c_col = c.reshape(bsz, FOX_HEADS, seq, 1)
            c_row = c.reshape(bsz, FOX_HEADS, seq // tq, tq)
            a_out = _fox_attn(qkv.reshape(bsz, seq, q_end), c_col, c_row, tq=tq)
            parts = [a_out.reshape(t, FOX_WIDTH), b_out]
            w_out = ab_w_out[j]
        else:
            proj, f_logit = _c_in(x2d, g_mix, c_w_in[j].astype(BF16))
            o = _hgrn(proj.reshape(bsz, seq, 4 * d), f_logit.reshape(bsz, seq, d),
                      c_lower_bounds, c_head_norm[j].reshape(1, d), layer_idx=j)
            parts = [o.reshape(t, d)]
            w_out = c_w_out[j]
        x2d = _post(parts, x2d, w_out.astype(BF16), norm_ffn[layer].reshape(1, d),
                    ffn_w_in[layer].astype(BF16), ffn_w_out[layer].astype(BF16), gf,
                    final=(layer == depth - 1))
    return x2d.reshape(bsz, seq, d)
```

```python
import functools

import jax
import jax.numpy as jnp
import numpy as np
from jax import lax
from jax.experimental import pallas as pl
from jax.experimental.pallas import tpu as pltpu

D_MODEL = 1024
EPS = 1e-6
NEG_BIG = -1e30

FOX_HEADS = 8
FOX_HEAD_DIM = 64
FOX_WIDTH = FOX_HEADS * FOX_HEAD_DIM
CONV_WIDTH = D_MODEL - FOX_WIDTH
CONV_TAPS = 3
LOG2E = 1.4426950408889634
FOX_Q_SCALE = LOG2E * FOX_HEAD_DIM ** -0.5
HGRN_HEADS = 8
HGRN_HEAD_DIM = D_MODEL // HGRN_HEADS
HGRN_CHUNK = 64

LANES = 128
SUBLANES = 8
VMEM_LIMIT = 56 * 1024 * 1024

BF16 = jnp.bfloat16
F32 = jnp.float32

NT_DIMS = (((1,), (1,)), ((), ()))
TN_DIMS = (((0,), (0,)), ((), ()))


def _resident(shape):
    zeros = (0,) * len(shape)
    return pl.BlockSpec(shape, lambda *_: zeros, pipeline_mode=pl.Buffered(1))


def _params(*sem):
    return pltpu.CompilerParams(dimension_semantics=sem, vmem_limit_bytes=VMEM_LIMIT)


def _rms_norm(x, g):
    return x * lax.rsqrt(jnp.mean(x * x, axis=-1, keepdims=True) + EPS) * g


def _sigmoid(x):
    return 0.5 * jnp.tanh(0.5 * x) + 0.5


def _ab_in_kernel(x_ref, g_ref, wqkv_ref, wf_ref, wconv_ref, cw_ref,
                  qkv_ref, f_ref, b_ref, zbuf, *, tm, seq):
    i = pl.program_id(0)
    h = _rms_norm(x_ref[...], g_ref[...]).astype(BF16)
    qkv = jnp.dot(h, wqkv_ref[...], preferred_element_type=F32)
    qkv_ref[:, :FOX_WIDTH] = (qkv[:, :FOX_WIDTH] * FOX_Q_SCALE).astype(BF16)
    qkv_ref[:, FOX_WIDTH:] = qkv[:, FOX_WIDTH:].astype(BF16)
    f_ref[...] = jnp.dot(h, wf_ref[...], preferred_element_type=F32)
    u = jnp.dot(h, wconv_ref[...], preferred_element_type=F32)
    u_b = u[:, :CONV_WIDTH]
    z = u[:, CONV_WIDTH:2 * CONV_WIDTH] * u[:, 2 * CONV_WIDTH:]

    @pl.when((i * tm) % seq == 0)
    def _():
        zbuf[0:SUBLANES, :] = jnp.zeros((SUBLANES, CONV_WIDTH), F32)

    zbuf[SUBLANES:SUBLANES + tm, :] = z
    cw = cw_ref[...]
    y = (cw[0:1, :] * zbuf[SUBLANES - 2:SUBLANES - 2 + tm, :]
         + cw[1:2, :] * zbuf[SUBLANES - 1:SUBLANES - 1 + tm, :]
         + cw[2:3, :] * z)
    b_ref[...] = (u_b * y).astype(BF16)
    zbuf[0:SUBLANES, :] = zbuf[tm:tm + SUBLANES, :]


def _ab_in(x2d, g, wqkv, wf, wconv, cw, *, seq, tm=512):
    t = x2d.shape[0]
    kern = functools.partial(_ab_in_kernel, tm=tm, seq=seq)
    return pl.pallas_call(
        kern,
        grid=(t // tm,),
        in_specs=[
            pl.BlockSpec((tm, D_MODEL), lambda i: (i, 0)),
            _resident((1, D_MODEL)),
            _resident(wqkv.shape),
            _resident(wf.shape),
            _resident(wconv.shape),
            _resident(cw.shape),
        ],
        out_specs=[
            pl.BlockSpec((tm, 3 * FOX_WIDTH), lambda i: (i, 0)),
            pl.BlockSpec((tm, LANES), lambda i: (i, 0)),
            pl.BlockSpec((tm, CONV_WIDTH), lambda i: (i, 0)),
        ],
        out_shape=[
            jax.ShapeDtypeStruct((t, 3 * FOX_WIDTH), BF16),
            jax.ShapeDtypeStruct((t, LANES), F32),
            jax.ShapeDtypeStruct((t, CONV_WIDTH), BF16),
        ],
        scratch_shapes=[pltpu.VMEM((tm + SUBLANES, CONV_WIDTH), F32)],
        compiler_params=_params("arbitrary"),
        name="ab_in",
    )(x2d, g, wqkv, wf, wconv, cw)


def _fox_cumsum_kernel(f_ref, bias_ref, c_ref, *, seq):
    x = f_ref[...] + bias_ref[...]
    ls = jnp.minimum(x, 0.0) - jnp.log(1.0 + jnp.exp(-jnp.abs(x)))
    rows = ls.shape[0]
    lane = lax.broadcasted_iota(jnp.int32, (rows, LANES), 1)
    carry = jnp.zeros((rows, 1), F32)
    for j in range(seq // LANES):
        blk = ls[:, j * LANES:(j + 1) * LANES]
        shift = 1
        while shift < LANES:
            blk = blk + jnp.where(lane >= shift, pltpu.roll(blk, shift, 1), 0.0)
            shift *= 2
        blk = blk + carry
        c_ref[:, j * LANES:(j + 1) * LANES] = blk * LOG2E
        carry = blk[:, LANES - 1:LANES]


def _fox_cumsum(f_rows, bias_rows):
    rows, seq = f_rows.shape
    return pl.pallas_call(
        functools.partial(_fox_cumsum_kernel, seq=seq),
        out_shape=jax.ShapeDtypeStruct((rows, seq), F32),
        name="fox_cumsum",
    )(f_rows, bias_rows)


def _fox_attn_kernel(q_ref, k_ref, v_ref, ccol_ref, crow_ref, o_ref, *, tq):
    qi = pl.program_id(2)
    lane = lax.broadcasted_iota(jnp.int32, (1, LANES), 1)
    q = q_ref[0]
    row = lax.broadcasted_iota(jnp.int32, (tq, tq), 0)
    col = lax.broadcasted_iota(jnp.int32, (tq, tq), 1)
    causal = row >= col
    qh = [jnp.where((lane >= hh * FOX_HEAD_DIM) & (lane < (hh + 1) * FOX_HEAD_DIM),
                    q, jnp.zeros_like(q)) for hh in range(2)]
    ct = [ccol_ref[0, hh] for hh in range(2)]

    def attend(n_kv):
        s = [[], []]
        for j in range(n_kv):
            k = k_ref[0, j * tq:(j + 1) * tq, :]
            for hh in range(2):
                sh = lax.dot_general(qh[hh], k, NT_DIMS, preferred_element_type=F32)
                sh = sh - crow_ref[0, hh, j:j + 1, :]
                s[hh].append(jnp.where(causal, sh, NEG_BIG) if j == n_kv - 1 else sh)
        shift = []
        for hh in range(2):
            mx = None
            for sj in s[hh]:
                for cb in range(tq // LANES):
                    chunk = sj[:, cb * LANES:(cb + 1) * LANES]
                    mx = chunk if mx is None else jnp.maximum(mx, chunk)
            m = jnp.max(mx, axis=-1, keepdims=True) + ct[hh]
            shift.append(m - ct[hh])
        acc = [None, None]
        for j in range(n_kv):
            v = v_ref[0, j * tq:(j + 1) * tq, :]
            v_aug = jnp.concatenate([v, jnp.ones_like(v)], axis=1)
            for hh in range(2):
                p = jnp.exp2(s[hh][j] - shift[hh]).astype(BF16)
                d = jnp.dot(p, v_aug, preferred_element_type=F32)
                acc[hh] = d if acc[hh] is None else acc[hh] + d
        out = [acc[hh][:, :LANES] / acc[hh][:, LANES:] for hh in range(2)]
        o_ref[0] = jnp.where(lane < FOX_HEAD_DIM, out[0], out[1]).astype(o_ref.dtype)

    for c in range(k_ref.shape[1] // tq):
        pl.when(qi == c)(functools.partial(attend, c + 1))


def _fox_attn(qkv, c_col, c_row, *, tq=256):
    bsz, seq, _ = qkv.shape
    n_pairs = FOX_WIDTH // LANES
    nblk = seq // tq
    return pl.pallas_call(
        functools.partial(_fox_attn_kernel, tq=tq),
        grid=(bsz, n_pairs, nblk),
        in_specs=[
            pl.BlockSpec((1, tq, LANES), lambda b, p, i: (b, i, p)),
            pl.BlockSpec((1, seq, LANES), lambda b, p, i: (b, 0, n_pairs + p)),
            pl.BlockSpec((1, seq, LANES), lambda b, p, i: (b, 0, 2 * n_pairs + p)),
            pl.BlockSpec((1, 2, tq, 1), lambda b, p, i: (b, p, i, 0)),
            pl.BlockSpec((1, 2, nblk, tq), lambda b, p, i: (b, p, 0, 0)),
        ],
        out_specs=pl.BlockSpec((1, tq, LANES), lambda b, p, i: (b, i, p)),
        out_shape=jax.ShapeDtypeStruct((bsz, seq, FOX_WIDTH), BF16),
        compiler_params=_params("parallel", "parallel", "arbitrary"),
        name="fox_attn",
    )(qkv, qkv, qkv, c_col, c_row)


def _post_kernel(*refs, n_parts, final):
    parts = refs[:n_parts]
    x_ref, wo_ref, g_ref, w1_ref, w2_ref, gf_ref, o_ref = refs[n_parts:]
    hidden = w2_ref.shape[0]
    mix = None
    off = 0
    for p_ref in parts:
        width = p_ref.shape[1]
        d = jnp.dot(p_ref[...], wo_ref[off:off + width, :], preferred_element_type=F32)
        mix = d if mix is None else mix + d
        off += width
    x1 = x_ref[...] + mix
    h = _rms_norm(x1, g_ref[...]).astype(BF16)
    gu = jnp.dot(h, w1_ref[...], preferred_element_type=F32)
    gate = gu[:, :hidden]
    act = (gate * _sigmoid(gate) * gu[:, hidden:]).astype(BF16)
    x2 = x1 + jnp.dot(act, w2_ref[...], preferred_element_type=F32)
    if final:
        x2 = _rms_norm(x2, gf_ref[...])
    o_ref[...] = x2


def _post(parts, x2d, wo, g, w1, w2, gf, *, final, tm=256):
    t = x2d.shape[0]
    kern = functools.partial(_post_kernel, n_parts=len(parts), final=final)
    in_specs = [pl.BlockSpec((tm, p.shape[1]), lambda i: (i, 0)) for p in parts]
    in_specs += [
        pl.BlockSpec((tm, D_MODEL), lambda i: (i, 0)),
        _resident(wo.shape),
        _resident((1, D_MODEL)),
        _resident(w1.shape),
        _resident(w2.shape),
        _resident((1, D_MODEL)),
    ]
    return pl.pallas_call(
        kern,
        grid=(t // tm,),
        in_specs=in_specs,
        out_specs=pl.BlockSpec((tm, D_MODEL), lambda i: (i, 0)),
        out_shape=jax.ShapeDtypeStruct((t, D_MODEL), F32),
        compiler_params=_params("parallel"),
        name="post",
    )(*parts, x2d, wo, g, w1, w2, gf)


def _c_in_kernel(x_ref, g_ref, w_ref, o_ref, f_ref):
    h = _rms_norm(x_ref[...], g_ref[...]).astype(BF16)
    pr = jnp.dot(h, w_ref[...], preferred_element_type=F32)
    o_ref[:, :D_MODEL] = pr[:, :D_MODEL].astype(BF16)
    o_ref[:, D_MODEL:] = pr[:, 2 * D_MODEL:].astype(BF16)
    f_ref[...] = pr[:, D_MODEL:2 * D_MODEL]


def _c_in(x2d, g, w, *, tm=512):
    t = x2d.shape[0]
    return pl.pallas_call(
        _c_in_kernel,
        grid=(t // tm,),
        in_specs=[
            pl.BlockSpec((tm, D_MODEL), lambda i: (i, 0)),
            _resident((1, D_MODEL)),
            _resident(w.shape),
        ],
        out_specs=[
            pl.BlockSpec((tm, 3 * D_MODEL), lambda i: (i, 0)),
            pl.BlockSpec((tm, D_MODEL), lambda i: (i, 0)),
        ],
        out_shape=[
            jax.ShapeDtypeStruct((t, 3 * D_MODEL), BF16),
            jax.ShapeDtypeStruct((t, D_MODEL), F32),
        ],
        compiler_params=_params("parallel"),
        name="c_in",
    )(x2d, g, w)


def _hgrn_levels():
    h, out = HGRN_CHUNK // 2, []
    while h >= 1:
        out.append(h)
        h //= 2
    return out


def _hgrn_cum_table():
    t = np.arange(HGRN_CHUNK)
    m = (t[None, :] <= t[:, None]).astype(np.float32)
    return np.concatenate([m, m, m], axis=1)


def _hgrn_kernel(q_ref, f_ref, i_ref, g_ref, lbraw_ref, hn_ref, cum_ref, o_ref, state, *,
                 layer_idx, blk):
    @pl.when(pl.program_id(2) == 0)
    def _():
        state[...] = jnp.zeros_like(state)

    raw = lbraw_ref[...]
    e = jnp.exp(raw - jnp.max(raw, axis=0, keepdims=True))
    sm = e / jnp.sum(e, axis=0, keepdims=True)
    lb = jnp.sum(sm[0:layer_idx + 1, :], axis=0, keepdims=True) - sm[0:1, :]
    hn = hn_ref[...]
    cum = cum_ref[...]

    c = HGRN_CHUNK
    levels = _hgrn_levels()
    row = lax.broadcasted_iota(jnp.int32, (c, HGRN_HEAD_DIM), 0)
    r2 = lax.broadcasted_iota(jnp.int32, (c, c), 0)
    c2 = lax.broadcasted_iota(jnp.int32, (c, c), 1)
    upper = [(row & h) != 0 for h in levels]
    pair = [((r2 // (2 * h)) == (c2 // (2 * h))) & ((r2 & h) != 0) & ((c2 & h) == 0)
            for h in levels]

    n = blk // c
    chunk = lambda arr, i: arr[i * c:(i + 1) * c]
    q = q_ref[0].astype(F32)
    v = i_ref[0]
    f = lb + (1.0 - lb) * _sigmoid(f_ref[0])
    k = 1.0 - f
    lg = jnp.log2(f)
    hi = lg.astype(BF16)
    r1 = lg - hi.astype(F32)
    mid = r1.astype(BF16)
    lo = (r1 - mid.astype(F32)).astype(BF16)
    b = [jnp.dot(cum, jnp.concatenate([chunk(hi, i), chunk(mid, i), chunk(lo, i)], axis=0),
                 preferred_element_type=F32) for i in range(n)]

    a = [None] * n
    for li, h in enumerate(levels):
        xs = []
        for i in range(n):
            qi, ki, bi = chunk(q, i), chunk(k, i), b[i]
            if h == 1:
                x = jnp.where(upper[li], qi * chunk(f, i), ki)
            else:
                if h >= SUBLANES:
                    ref = jnp.concatenate(
                        [jnp.broadcast_to(bi[base + h - 1:base + h, :], (2 * h, HGRN_HEAD_DIM))
                         for base in range(0, c, 2 * h)], axis=0)
                else:
                    b3 = bi.reshape(c // SUBLANES, SUBLANES, HGRN_HEAD_DIM)
                    refs = [jnp.broadcast_to(b3[:, base + h - 1:base + h, :], b3.shape).reshape(bi.shape)
                            for base in range(0, SUBLANES, 2 * h)]
                    ref = refs[0]
                    for m in range(1, len(refs)):
                        ref = jnp.where((row % SUBLANES) >= m * 2 * h, refs[m], ref)
                x = jnp.where(upper[li], qi, ki) * jnp.exp2(-jnp.abs(bi - ref))
            xs.append(x.astype(BF16))
        ps = [lax.dot_general(x, x, NT_DIMS, preferred_element_type=F32) for x in xs]
        a = [jnp.where(pair[li], ps[i], 0.0 if a[i] is None else a[i]) for i in range(n)]
    o_intra = [jnp.dot(a[i].astype(BF16), chunk(v, i), preferred_element_type=F32) for i in range(n)]

    b_last = [b[i][c - 1:c, :] for i in range(n)]
    kv = [lax.dot_general(chunk(v, i), (chunk(k, i) * jnp.exp2(b_last[i] - b[i])).astype(BF16),
                          TN_DIMS, preferred_element_type=F32) for i in range(n)]
    st = state[...]
    o_inter = []
    for i in range(n):
        o_inter.append(lax.dot_general((chunk(q, i) * jnp.exp2(b[i])).astype(BF16), st.astype(BF16),
                                       NT_DIMS, preferred_element_type=F32))
        st = st * jnp.exp2(b_last[i]) + kv[i]
    state[...] = st

    o = jnp.concatenate([o_inter[i] + o_intra[i] for i in range(n)], axis=0)
    o = o + jnp.sum(q * k, axis=-1, keepdims=True) * v.astype(F32)
    o = o * lax.rsqrt(jnp.mean(o * o, axis=-1, keepdims=True) + EPS) * hn
    g = g_ref[0].astype(F32)
    o_ref[0] = (o * (g * _sigmoid(g))).astype(o_ref.dtype)


def _hgrn(proj, f_logit, lb_raw, head_norm, *, layer_idx, blk=1024):
    bsz, seq, _ = f_logit.shape
    hd = HGRN_HEAD_DIM
    nh = HGRN_HEADS
    cum = jnp.asarray(_hgrn_cum_table(), BF16)
    kern = functools.partial(_hgrn_kernel, layer_idx=layer_idx, blk=blk)
    return pl.pallas_call(
        kern,
        grid=(bsz, nh, seq // blk),
        in_specs=[
            pl.BlockSpec((1, blk, hd), lambda b, h, l: (b, l, h)),
            pl.BlockSpec((1, blk, hd), lambda b, h, l: (b, l, h)),
            pl.BlockSpec((1, blk, hd), lambda b, h, l: (b, l, nh + h)),
            pl.BlockSpec((1, blk, hd), lambda b, h, l: (b, l, 2 * nh + h)),
            pl.BlockSpec((lb_raw.shape[0], hd), lambda b, h, l: (0, h)),
            pl.BlockSpec((1, hd), lambda b, h, l: (0, h)),
            _resident(cum.shape),
        ],
        out_specs=pl.BlockSpec((1, blk, hd), lambda b, h, l: (b, l, h)),
        out_shape=jax.ShapeDtypeStruct((bsz, seq, D_MODEL), BF16),
        scratch_shapes=[pltpu.VMEM((hd, hd), F32)],
        compiler_params=_params("parallel", "parallel", "arbitrary"),
        name="hgrn",
    )(proj, f_logit, proj, proj, lb_raw, head_norm, cum)


def kernel(x, norm_mix, norm_ffn, final_norm, ab_w_in, fox_f_bias, conv_w, ab_w_out,
           c_w_in, c_lower_bounds, c_head_norm, c_w_out, ffn_w_in, ffn_w_out):
    bsz, seq, d = x.shape
    depth = norm_mix.shape[0]
    t = bsz * seq
    x2d = x.reshape(t, d)
    gf = final_norm.reshape(1, d)
    q_end = 3 * FOX_WIDTH
    f_end = q_end + FOX_HEADS

    for layer in range(depth):
        j = layer // 2
        g_mix = norm_mix[layer].reshape(1, d)
        if layer % 2 == 0:
            w_in = ab_w_in[j]
            wqkv = w_in[:, :q_end].astype(BF16)
            wf = jnp.pad(w_in[:, q_end:f_end], ((0, 0), (0, LANES - FOX_HEADS))).astype(BF16)
            wconv = w_in[:, f_end:].astype(BF16)
            qkv, f_pad, b_out = _ab_in(x2d, g_mix, wqkv, wf, wconv, conv_w[j], seq=seq)
            f_rows = f_pad[:, :FOX_HEADS].reshape(bsz, seq, FOX_HEADS)
            f_rows = jnp.transpose(f_rows, (0, 2, 1)).reshape(bsz * FOX_HEADS, seq)
            bias_rows = jnp.tile(fox_f_bias[j], bsz).reshape(bsz * FOX_HEADS, 1)
            c = _fox_cumsum(f_rows, bias_rows)
            tq = 256
            c_col = c.reshape(bsz, FOX_HEADS, seq, 1)
            c_row = c.reshape(bsz, FOX_HEADS, seq // tq, tq)
            a_out = _fox_attn(qkv.reshape(bsz, seq, q_end), c_col, c_row, tq=tq)
            parts = [a_out.reshape(t, FOX_WIDTH), b_out]
            w_out = ab_w_out[j]
        else:
            proj, f_logit = _c_in(x2d, g_mix, c_w_in[j].astype(BF16))
            o = _hgrn(proj.reshape(bsz, seq, 3 * d), f_logit.reshape(bsz, seq, d),
                      c_lower_bounds, c_head_norm[j].reshape(1, d), layer_idx=j)
            parts = [o.reshape(t, d)]
            w_out = c_w_out[j]
        x2d = _post(parts, x2d, w_out.astype(BF16), norm_ffn[layer].reshape(1, d),
                    ffn_w_in[layer].astype(BF16), ffn_w_out[layer].astype(BF16), gf,
                    final=(layer == depth - 1))
    return x2d.reshape(bsz, seq, d)
```

```python
import functools

import jax
import jax.numpy as jnp
import numpy as np
from jax import lax
from jax.experimental import pallas as pl
from jax.experimental.pallas import tpu as pltpu

D_MODEL = 1024
EPS = 1e-6
NEG_BIG = -1e30

FOX_HEADS = 8
FOX_HEAD_DIM = 64
FOX_WIDTH = FOX_HEADS * FOX_HEAD_DIM
CONV_WIDTH = D_MODEL - FOX_WIDTH
CONV_TAPS = 3
LOG2E = 1.4426950408889634
FOX_Q_SCALE = LOG2E * FOX_HEAD_DIM ** -0.5
FOX_Q_TILE = 256
HGRN_HEADS = 8
HGRN_HEAD_DIM = D_MODEL // HGRN_HEADS
HGRN_CHUNK = 64

LANES = 128
SUBLANES = 8
VMEM_LIMIT = 56 * 1024 * 1024

BF16 = jnp.bfloat16
F32 = jnp.float32

NT_DIMS = (((1,), (1,)), ((), ()))
TN_DIMS = (((0,), (0,)), ((), ()))


def _resident(shape):
    zeros = (0,) * len(shape)
    return pl.BlockSpec(shape, lambda *_: zeros, pipeline_mode=pl.Buffered(1))


def _params(*sem):
    return pltpu.CompilerParams(dimension_semantics=sem, vmem_limit_bytes=VMEM_LIMIT)


def _rms_norm(x, g):
    return x * lax.rsqrt(jnp.mean(x * x, axis=-1, keepdims=True) + EPS) * g


def _sigmoid(x):
    return 0.5 * jnp.tanh(0.5 * x) + 0.5


def _ab_in_kernel(x_ref, g_ref, wqkv_ref, wf_ref, wconv_ref, cw_ref,
                  qkv_ref, f_ref, b_ref, zbuf, *, tm, seq):
    i = pl.program_id(0)
    h = _rms_norm(x_ref[...], g_ref[...]).astype(BF16)
    qkv = jnp.dot(h, wqkv_ref[...], preferred_element_type=F32)
    qkv_ref[:, :FOX_WIDTH] = (qkv[:, :FOX_WIDTH] * FOX_Q_SCALE).astype(BF16)
    qkv_ref[:, FOX_WIDTH:] = qkv[:, FOX_WIDTH:].astype(BF16)
    f_ref[...] = jnp.dot(h, wf_ref[...], preferred_element_type=F32)
    u = jnp.dot(h, wconv_ref[...], preferred_element_type=F32)
    u_b = u[:, :CONV_WIDTH]
    z = u[:, CONV_WIDTH:2 * CONV_WIDTH] * u[:, 2 * CONV_WIDTH:]

    @pl.when((i * tm) % seq == 0)
    def _():
        zbuf[0:SUBLANES, :] = jnp.zeros((SUBLANES, CONV_WIDTH), F32)

    zbuf[SUBLANES:SUBLANES + tm, :] = z
    cw = cw_ref[...]
    y = (cw[0:1, :] * zbuf[SUBLANES - 2:SUBLANES - 2 + tm, :]
         + cw[1:2, :] * zbuf[SUBLANES - 1:SUBLANES - 1 + tm, :]
         + cw[2:3, :] * z)
    b_ref[...] = (u_b * y).astype(BF16)
    zbuf[0:SUBLANES, :] = zbuf[tm:tm + SUBLANES, :]


def _ab_in(x2d, g, wqkv, wf, wconv, cw, *, seq, tm=512):
    t = x2d.shape[0]
    kern = functools.partial(_ab_in_kernel, tm=tm, seq=seq)
    return pl.pallas_call(
        kern,
        grid=(t // tm,),
        in_specs=[
            pl.BlockSpec((tm, D_MODEL), lambda i: (i, 0)),
            _resident((1, D_MODEL)),
            _resident(wqkv.shape),
            _resident(wf.shape),
            _resident(wconv.shape),
            _resident(cw.shape),
        ],
        out_specs=[
            pl.BlockSpec((tm, 3 * FOX_WIDTH), lambda i: (i, 0)),
            pl.BlockSpec((tm, LANES), lambda i: (i, 0)),
            pl.BlockSpec((tm, CONV_WIDTH), lambda i: (i, 0)),
        ],
        out_shape=[
            jax.ShapeDtypeStruct((t, 3 * FOX_WIDTH), BF16),
            jax.ShapeDtypeStruct((t, LANES), F32),
            jax.ShapeDtypeStruct((t, CONV_WIDTH), BF16),
        ],
        scratch_shapes=[pltpu.VMEM((tm + SUBLANES, CONV_WIDTH), F32)],
        compiler_params=_params("arbitrary"),
        name="ab_in",
    )(x2d, g, wqkv, wf, wconv, cw)


def _fox_cumsum_kernel(f_ref, bias_ref, c_ref, *, seq):
    x = f_ref[...] + bias_ref[...]
    ls = jnp.minimum(x, 0.0) - jnp.log(1.0 + jnp.exp(-jnp.abs(x)))
    rows = ls.shape[0]
    lane = lax.broadcasted_iota(jnp.int32, (rows, LANES), 1)
    carry = jnp.zeros((rows, 1), F32)
    for j in range(seq // LANES):
        blk = ls[:, j * LANES:(j + 1) * LANES]
        shift = 1
        while shift < LANES:
            blk = blk + jnp.where(lane >= shift, pltpu.roll(blk, shift, 1), 0.0)
            shift *= 2
        blk = blk + carry
        c_ref[:, j * LANES:(j + 1) * LANES] = blk * LOG2E
        carry = blk[:, LANES - 1:LANES]


def _fox_cumsum(f_rows, bias_rows):
    rows, seq = f_rows.shape
    return pl.pallas_call(
        functools.partial(_fox_cumsum_kernel, seq=seq),
        out_shape=jax.ShapeDtypeStruct((rows, seq), F32),
        name="fox_cumsum",
    )(f_rows, bias_rows)


def _fox_attn_kernel(q_ref, k_ref, v_ref, crow_ref, o_ref, *, tq):
    i = pl.program_id(2)
    n_half = k_ref.shape[1] // tq // 2
    lane = lax.broadcasted_iota(jnp.int32, (1, LANES), 1)
    row = lax.broadcasted_iota(jnp.int32, (tq, tq), 0)
    col = lax.broadcasted_iota(jnp.int32, (tq, tq), 1)
    causal = row >= col
    in_head = [(lane >= hh * FOX_HEAD_DIM) & (lane < (hh + 1) * FOX_HEAD_DIM) for hh in range(2)]

    def attend(first):
        probs = [(g, first + g * n_half, hh) for g in range(2) for hh in range(2)]
        s = []
        for g, c, hh in probs:
            q = q_ref[0, g]
            qh = jnp.where(in_head[hh], q, jnp.zeros_like(q))
            sj = []
            for j in range(c + 1):
                sh = lax.dot_general(qh, k_ref[0, j * tq:(j + 1) * tq, :], NT_DIMS,
                                     preferred_element_type=F32)
                sh = sh - crow_ref[0, hh, j:j + 1, :]
                sj.append(jnp.where(causal, sh, NEG_BIG) if j == c else sh)
            s.append(sj)
        shift = []
        for (g, c, hh), sj in zip(probs, s):
            mx = None
            for blk in sj:
                for cb in range(tq // LANES):
                    part = blk[:, cb * LANES:(cb + 1) * LANES]
                    mx = part if mx is None else jnp.maximum(mx, part)
            ct = jnp.transpose(jnp.broadcast_to(crow_ref[0, hh, c:c + 1, :], (SUBLANES, tq)))[:, 0:1]
            m = jnp.max(mx, axis=-1, keepdims=True) + ct
            shift.append(m - ct)
        outs = []
        for (g, c, hh), sj, sh in zip(probs, s, shift):
            acc = None
            for j in range(c + 1):
                v = v_ref[0, j * tq:(j + 1) * tq, :]
                v_aug = jnp.concatenate([v, jnp.ones_like(v)], axis=1)
                d = jnp.dot(jnp.exp2(sj[j] - sh).astype(BF16), v_aug, preferred_element_type=F32)
                acc = d if acc is None else acc + d
            outs.append(acc[:, :LANES] / acc[:, LANES:])
        for g in range(2):
            o_ref[0, g] = jnp.where(lane < FOX_HEAD_DIM, outs[2 * g], outs[2 * g + 1]).astype(o_ref.dtype)

    for c in range(n_half):
        pl.when(i == c)(functools.partial(attend, c))


def _fox_attn(qkv, c_row, *, tq):
    bsz, seq, width = qkv.shape
    n_pairs = FOX_WIDTH // LANES
    nblk = seq // tq
    half = seq // 2
    qkv_halves = qkv.reshape(bsz, 2, half, width)
    out = pl.pallas_call(
        functools.partial(_fox_attn_kernel, tq=tq),
        grid=(bsz, n_pairs, nblk // 2),
        in_specs=[
            pl.BlockSpec((1, 2, tq, LANES), lambda b, p, i: (b, 0, i, p)),
            pl.BlockSpec((1, seq, LANES), lambda b, p, i: (b, 0, n_pairs + p)),
            pl.BlockSpec((1, seq, LANES), lambda b, p, i: (b, 0, 2 * n_pairs + p)),
            pl.BlockSpec((1, 2, nblk, tq), lambda b, p, i: (b, p, 0, 0)),
        ],
        out_specs=pl.BlockSpec((1, 2, tq, LANES), lambda b, p, i: (b, 0, i, p)),
        out_shape=jax.ShapeDtypeStruct((bsz, 2, half, FOX_WIDTH), BF16),
        compiler_params=_params("parallel", "parallel", "arbitrary"),
        name="fox_attn",
    )(qkv_halves, qkv, qkv, c_row)
    return out.reshape(bsz, seq, FOX_WIDTH)


def _post_kernel(*refs, n_parts, final):
    parts = refs[:n_parts]
    x_ref, wo_ref, g_ref, w1_ref, w2_ref, gf_ref, o_ref = refs[n_parts:]
    hidden = w2_ref.shape[0]
    mix = None
    off = 0
    for p_ref in parts:
        width = p_ref.shape[1]
        d = jnp.dot(p_ref[...], wo_ref[off:off + width, :], preferred_element_type=F32)
        mix = d if mix is None else mix + d
        off += width
    x1 = x_ref[...] + mix
    h = _rms_norm(x1, g_ref[...]).astype(BF16)
    gu = jnp.dot(h, w1_ref[...], preferred_element_type=F32)
    gate = gu[:, :hidden]
    act = (gate * _sigmoid(gate) * gu[:, hidden:]).astype(BF16)
    x2 = x1 + jnp.dot(act, w2_ref[...], preferred_element_type=F32)
    if final:
        x2 = _rms_norm(x2, gf_ref[...])
    o_ref[...] = x2


def _post(parts, x2d, wo, g, w1, w2, gf, *, final, tm=256):
    t = x2d.shape[0]
    kern = functools.partial(_post_kernel, n_parts=len(parts), final=final)
    in_specs = [pl.BlockSpec((tm, p.shape[1]), lambda i: (i, 0)) for p in parts]
    in_specs += [
        pl.BlockSpec((tm, D_MODEL), lambda i: (i, 0)),
        _resident(wo.shape),
        _resident((1, D_MODEL)),
        _resident(w1.shape),
        _resident(w2.shape),
        _resident((1, D_MODEL)),
    ]
    return pl.pallas_call(
        kern,
        grid=(t // tm,),
        in_specs=in_specs,
        out_specs=pl.BlockSpec((tm, D_MODEL), lambda i: (i, 0)),
        out_shape=jax.ShapeDtypeStruct((t, D_MODEL), F32),
        compiler_params=_params("parallel"),
        name="post",
    )(*parts, x2d, wo, g, w1, w2, gf)


def _c_in_kernel(x_ref, g_ref, w_ref, o_ref, f_ref):
    h = _rms_norm(x_ref[...], g_ref[...]).astype(BF16)
    pr = jnp.dot(h, w_ref[...], preferred_element_type=F32)
    o_ref[:, :D_MODEL] = pr[:, :D_MODEL].astype(BF16)
    o_ref[:, D_MODEL:] = pr[:, 2 * D_MODEL:].astype(BF16)
    f_ref[...] = pr[:, D_MODEL:2 * D_MODEL]


def _c_in(x2d, g, w, *, tm=512):
    t = x2d.shape[0]
    return pl.pallas_call(
        _c_in_kernel,
        grid=(t // tm,),
        in_specs=[
            pl.BlockSpec((tm, D_MODEL), lambda i: (i, 0)),
            _resident((1, D_MODEL)),
            _resident(w.shape),
        ],
        out_specs=[
            pl.BlockSpec((tm, 3 * D_MODEL), lambda i: (i, 0)),
            pl.BlockSpec((tm, D_MODEL), lambda i: (i, 0)),
        ],
        out_shape=[
            jax.ShapeDtypeStruct((t, 3 * D_MODEL), BF16),
            jax.ShapeDtypeStruct((t, D_MODEL), F32),
        ],
        compiler_params=_params("parallel"),
        name="c_in",
    )(x2d, g, w)


def _hgrn_levels():
    h, out = HGRN_CHUNK // 2, []
    while h >= 1:
        out.append(h)
        h //= 2
    return out


def _hgrn_cum_table():
    t = np.arange(HGRN_CHUNK)
    m = (t[None, :] <= t[:, None]).astype(np.float32)
    return np.concatenate([m, m, m], axis=1)


def _hgrn_kernel(q_ref, f_ref, i_ref, g_ref, lbraw_ref, hn_ref, cum_ref, o_ref, state, *,
                 layer_idx, blk):
    @pl.when(pl.program_id(2) == 0)
    def _():
        state[...] = jnp.zeros_like(state)

    raw = lbraw_ref[...]
    e = jnp.exp(raw - jnp.max(raw, axis=0, keepdims=True))
    sm = e / jnp.sum(e, axis=0, keepdims=True)
    lb = jnp.sum(sm[0:layer_idx + 1, :], axis=0, keepdims=True) - sm[0:1, :]
    hn = hn_ref[...]
    cum = cum_ref[...]

    c = HGRN_CHUNK
    levels = _hgrn_levels()
    row = lax.broadcasted_iota(jnp.int32, (c, HGRN_HEAD_DIM), 0)
    r2 = lax.broadcasted_iota(jnp.int32, (c, c), 0)
    c2 = lax.broadcasted_iota(jnp.int32, (c, c), 1)
    upper = [(row & h) != 0 for h in levels]
    pair = [((r2 // (2 * h)) == (c2 // (2 * h))) & ((r2 & h) != 0) & ((c2 & h) == 0)
            for h in levels]

    n = blk // c
    chunk = lambda arr, i: arr[i * c:(i + 1) * c]
    q = q_ref[0].astype(F32)
    v = i_ref[0]
    f = lb + (1.0 - lb) * _sigmoid(f_ref[0])
    k = 1.0 - f
    lg = jnp.log2(f)
    hi = lg.astype(BF16)
    r1 = lg - hi.astype(F32)
    mid = r1.astype(BF16)
    lo = (r1 - mid.astype(F32)).astype(BF16)
    b = [jnp.dot(cum, jnp.concatenate([chunk(hi, i), chunk(mid, i), chunk(lo, i)], axis=0),
                 preferred_element_type=F32) for i in range(n)]

    a = [None] * n
    for li, h in enumerate(levels):
        xs = []
        for i in range(n):
            qi, ki, bi = chunk(q, i), chunk(k, i), b[i]
            if h == 1:
                x = jnp.where(upper[li], qi * chunk(f, i), ki)
            else:
                if h >= SUBLANES:
                    ref = jnp.concatenate(
                        [jnp.broadcast_to(bi[base + h - 1:base + h, :], (2 * h, HGRN_HEAD_DIM))
                         for base in range(0, c, 2 * h)], axis=0)
                else:
                    b3 = bi.reshape(c // SUBLANES, SUBLANES, HGRN_HEAD_DIM)
                    refs = [jnp.broadcast_to(b3[:, base + h - 1:base + h, :], b3.shape).reshape(bi.shape)
                            for base in range(0, SUBLANES, 2 * h)]
                    ref = refs[0]
                    for m in range(1, len(refs)):
                        ref = jnp.where((row % SUBLANES) >= m * 2 * h, refs[m], ref)
                x = jnp.where(upper[li], qi, ki) * jnp.exp2(-jnp.abs(bi - ref))
            xs.append(x.astype(BF16))
        ps = [lax.dot_general(x, x, NT_DIMS, preferred_element_type=F32) for x in xs]
        a = [jnp.where(pair[li], ps[i], 0.0 if a[i] is None else a[i]) for i in range(n)]
    o_intra = [jnp.dot(a[i].astype(BF16), chunk(v, i), preferred_element_type=F32) for i in range(n)]

    b_last = [b[i][c - 1:c, :] for i in range(n)]
    kv = [lax.dot_general(chunk(v, i), (chunk(k, i) * jnp.exp2(b_last[i] - b[i])).astype(BF16),
                          TN_DIMS, preferred_element_type=F32) for i in range(n)]
    st = state[...]
    o_inter = []
    for i in range(n):
        o_inter.append(lax.dot_general((chunk(q, i) * jnp.exp2(b[i])).astype(BF16), st.astype(BF16),
                                       NT_DIMS, preferred_element_type=F32))
        st = st * jnp.exp2(b_last[i]) + kv[i]
    state[...] = st

    o = jnp.concatenate([o_inter[i] + o_intra[i] for i in range(n)], axis=0)
    o = o + jnp.sum(q * k, axis=-1, keepdims=True) * v.astype(F32)
    o = o * lax.rsqrt(jnp.mean(o * o, axis=-1, keepdims=True) + EPS) * hn
    g = g_ref[0].astype(F32)
    o_ref[0] = (o * (g * _sigmoid(g))).astype(o_ref.dtype)


def _hgrn(proj, f_logit, lb_raw, head_norm, *, layer_idx, blk=1024):
    bsz, seq, _ = f_logit.shape
    hd = HGRN_HEAD_DIM
    nh = HGRN_HEADS
    cum = jnp.asarray(_hgrn_cum_table(), BF16)
    kern = functools.partial(_hgrn_kernel, layer_idx=layer_idx, blk=blk)
    return pl.pallas_call(
        kern,
        grid=(bsz, nh, seq // blk),
        in_specs=[
            pl.BlockSpec((1, blk, hd), lambda b, h, l: (b, l, h)),
            pl.BlockSpec((1, blk, hd), lambda b, h, l: (b, l, h)),
            pl.BlockSpec((1, blk, hd), lambda b, h, l: (b, l, nh + h)),
            pl.BlockSpec((1, blk, hd), lambda b, h, l: (b, l, 2 * nh + h)),
            pl.BlockSpec((lb_raw.shape[0], hd), lambda b, h, l: (0, h)),
            pl.BlockSpec((1, hd), lambda b, h, l: (0, h)),
            _resident(cum.shape),
        ],
        out_specs=pl.BlockSpec((1, blk, hd), lambda b, h, l: (b, l, h)),
        out_shape=jax.ShapeDtypeStruct((bsz, seq, D_MODEL), BF16),
        scratch_shapes=[pltpu.VMEM((hd, hd), F32)],
        compiler_params=_params("parallel", "parallel", "arbitrary"),
        name="hgrn",
    )(proj, f_logit, proj, proj, lb_raw, head_norm, cum)


def kernel(x, norm_mix, norm_ffn, final_norm, ab_w_in, fox_f_bias, conv_w, ab_w_out,
           c_w_in, c_lower_bounds, c_head_norm, c_w_out, ffn_w_in, ffn_w_out):
    bsz, seq, d = x.shape
    depth = norm_mix.shape[0]
    t = bsz * seq
    x2d = x.reshape(t, d)
    gf = final_norm.reshape(1, d)
    q_end = 3 * FOX_WIDTH
    f_end = q_end + FOX_HEADS

    for layer in range(depth):
        j = layer // 2
        g_mix = norm_mix[layer].reshape(1, d)
        if layer % 2 == 0:
            w_in = ab_w_in[j]
            wqkv = w_in[:, :q_end].astype(BF16)
            wf = jnp.pad(w_in[:, q_end:f_end], ((0, 0), (0, LANES - FOX_HEADS))).astype(BF16)
            wconv = w_in[:, f_end:].astype(BF16)
            qkv, f_pad, b_out = _ab_in(x2d, g_mix, wqkv, wf, wconv, conv_w[j], seq=seq)
            f_rows = f_pad[:, :FOX_HEADS].reshape(bsz, seq, FOX_HEADS)
            f_rows = jnp.transpose(f_rows, (0, 2, 1)).reshape(bsz * FOX_HEADS, seq)
            bias_rows = jnp.tile(fox_f_bias[j], bsz).reshape(bsz * FOX_HEADS, 1)
            c = _fox_cumsum(f_rows, bias_rows)
            tq = FOX_Q_TILE
            c_row = c.reshape(bsz, FOX_HEADS, seq // tq, tq)
            a_out = _fox_attn(qkv.reshape(bsz, seq, q_end), c_row, tq=tq)
            parts = [a_out.reshape(t, FOX_WIDTH), b_out]
            w_out = ab_w_out[j]
        else:
            proj, f_logit = _c_in(x2d, g_mix, c_w_in[j].astype(BF16))
            o = _hgrn(proj.reshape(bsz, seq, 3 * d), f_logit.reshape(bsz, seq, d),
                      c_lower_bounds, c_head_norm[j].reshape(1, d), layer_idx=j)
            parts = [o.reshape(t, d)]
            w_out = c_w_out[j]
        x2d = _post(parts, x2d, w_out.astype(BF16), norm_ffn[layer].reshape(1, d),
                    ffn_w_in[layer].astype(BF16), ffn_w_out[layer].astype(BF16), gf,
                    final=(layer == depth - 1))
    return x2d.reshape(bsz, seq, d)
```

```python
import functools

import jax
import jax.numpy as jnp
import numpy as np
from jax import lax
from jax.experimental import pallas as pl
from jax.experimental.pallas import tpu as pltpu

D_MODEL = 1024
EPS = 1e-6
NEG_BIG = -1e30

FOX_HEADS = 8
FOX_HEAD_DIM = 64
FOX_WIDTH = FOX_HEADS * FOX_HEAD_DIM
CONV_WIDTH = D_MODEL - FOX_WIDTH
CONV_TAPS = 3
LOG2E = 1.4426950408889634
FOX_Q_SCALE = LOG2E * FOX_HEAD_DIM ** -0.5
FOX_Q_TILE = 256
HGRN_HEADS = 8
HGRN_HEAD_DIM = D_MODEL // HGRN_HEADS
HGRN_CHUNK = 64

LANES = 128
SUBLANES = 8
VMEM_LIMIT = 56 * 1024 * 1024

BF16 = jnp.bfloat16
F32 = jnp.float32

NT_DIMS = (((1,), (1,)), ((), ()))
TN_DIMS = (((0,), (0,)), ((), ()))


def _resident(shape):
    zeros = (0,) * len(shape)
    return pl.BlockSpec(shape, lambda *_: zeros, pipeline_mode=pl.Buffered(1))


def _params(*sem):
    return pltpu.CompilerParams(dimension_semantics=sem, vmem_limit_bytes=VMEM_LIMIT)


def _rms_norm(x, g):
    return x * lax.rsqrt(jnp.mean(x * x, axis=-1, keepdims=True) + EPS) * g


def _sigmoid(x):
    return 0.5 * jnp.tanh(0.5 * x) + 0.5


def _ab_in_kernel(x_ref, g_ref, wqkv_ref, wf_ref, wconv_ref, cw_ref,
                  qkv_ref, f_ref, b_ref, zbuf, *, tm, seq):
    i = pl.program_id(0)
    h = _rms_norm(x_ref[...], g_ref[...]).astype(BF16)
    qkv = jnp.dot(h, wqkv_ref[...], preferred_element_type=F32)
    qkv_ref[:, :FOX_WIDTH] = (qkv[:, :FOX_WIDTH] * FOX_Q_SCALE).astype(BF16)
    qkv_ref[:, FOX_WIDTH:] = qkv[:, FOX_WIDTH:].astype(BF16)
    f_ref[...] = jnp.dot(h, wf_ref[...], preferred_element_type=F32)
    u = jnp.dot(h, wconv_ref[...], preferred_element_type=F32)
    u_b = u[:, :CONV_WIDTH]
    z = u[:, CONV_WIDTH:2 * CONV_WIDTH] * u[:, 2 * CONV_WIDTH:]

    @pl.when((i * tm) % seq == 0)
    def _():
        zbuf[0:SUBLANES, :] = jnp.zeros((SUBLANES, CONV_WIDTH), F32)

    zbuf[SUBLANES:SUBLANES + tm, :] = z
    cw = cw_ref[...]
    y = (cw[0:1, :] * zbuf[SUBLANES - 2:SUBLANES - 2 + tm, :]
         + cw[1:2, :] * zbuf[SUBLANES - 1:SUBLANES - 1 + tm, :]
         + cw[2:3, :] * z)
    b_ref[...] = (u_b * y).astype(BF16)
    zbuf[0:SUBLANES, :] = zbuf[tm:tm + SUBLANES, :]


def _ab_in(x2d, g, wqkv, wf, wconv, cw, *, seq, tm=512):
    t = x2d.shape[0]
    kern = functools.partial(_ab_in_kernel, tm=tm, seq=seq)
    return pl.pallas_call(
        kern,
        grid=(t // tm,),
        in_specs=[
            pl.BlockSpec((tm, D_MODEL), lambda i: (i, 0)),
            _resident((1, D_MODEL)),
            _resident(wqkv.shape),
            _resident(wf.shape),
            _resident(wconv.shape),
            _resident(cw.shape),
        ],
        out_specs=[
            pl.BlockSpec((tm, 3 * FOX_WIDTH), lambda i: (i, 0)),
            pl.BlockSpec((tm, LANES), lambda i: (i, 0)),
            pl.BlockSpec((tm, CONV_WIDTH), lambda i: (i, 0)),
        ],
        out_shape=[
            jax.ShapeDtypeStruct((t, 3 * FOX_WIDTH), BF16),
            jax.ShapeDtypeStruct((t, LANES), F32),
            jax.ShapeDtypeStruct((t, CONV_WIDTH), BF16),
        ],
        scratch_shapes=[pltpu.VMEM((tm + SUBLANES, CONV_WIDTH), F32)],
        compiler_params=_params("arbitrary"),
        name="ab_in",
    )(x2d, g, wqkv, wf, wconv, cw)


def _fox_cumsum_kernel(f_ref, bias_ref, c_ref):
    heads, seq = c_ref.shape[1], c_ref.shape[2]
    lane = lax.broadcasted_iota(jnp.int32, (heads, LANES), 1)
    carry = jnp.zeros((heads, 1), F32)
    for j in range(seq // LANES):
        x = jnp.transpose(f_ref[0, j * LANES:(j + 1) * LANES, :])[0:heads, :] + bias_ref[...]
        blk = jnp.minimum(x, 0.0) - jnp.log(1.0 + jnp.exp(-jnp.abs(x)))
        shift = 1
        while shift < LANES:
            blk = blk + jnp.where(lane >= shift, pltpu.roll(blk, shift, 1), 0.0)
            shift *= 2
        blk = blk + carry
        c_ref[0, :, j * LANES:(j + 1) * LANES] = blk * LOG2E
        carry = blk[:, LANES - 1:LANES]


def _fox_cumsum(f_tok, bias_col):
    bsz, seq, _ = f_tok.shape
    heads = bias_col.shape[0]
    return pl.pallas_call(
        _fox_cumsum_kernel,
        grid=(bsz,),
        in_specs=[pl.BlockSpec((1, seq, LANES), lambda b: (b, 0, 0)), _resident(bias_col.shape)],
        out_specs=pl.BlockSpec((1, heads, seq), lambda b: (b, 0, 0)),
        out_shape=jax.ShapeDtypeStruct((bsz, heads, seq), F32),
        compiler_params=_params("parallel"),
        name="fox_cumsum",
    )(f_tok, bias_col)


def _fox_attn_kernel(q_ref, k_ref, v_ref, crow_ref, o_ref, *, tq):
    nblk = k_ref.shape[1] // tq
    lane = lax.broadcasted_iota(jnp.int32, (1, LANES), 1)
    row = lax.broadcasted_iota(jnp.int32, (tq, tq), 0)
    col = lax.broadcasted_iota(jnp.int32, (tq, tq), 1)
    causal = row >= col
    in_head = [(lane >= hh * FOX_HEAD_DIM) & (lane < (hh + 1) * FOX_HEAD_DIM) for hh in range(2)]
    rows = lambda ref, c: ref[0, c * tq:(c + 1) * tq, :]

    for first in range(nblk // 2):
        probs = [(c, hh) for c in (first, nblk - 1 - first) for hh in range(2)]
        s = []
        for c, hh in probs:
            q = rows(q_ref, c)
            qh = jnp.where(in_head[hh], q, jnp.zeros_like(q))
            sj = []
            for j in range(c + 1):
                sh = lax.dot_general(qh, rows(k_ref, j), NT_DIMS, preferred_element_type=F32)
                sh = sh - crow_ref[0, hh, j:j + 1, :]
                sj.append(jnp.where(causal, sh, NEG_BIG) if j == c else sh)
            s.append(sj)
        shift = []
        for (c, hh), sj in zip(probs, s):
            mx = None
            for blk in sj:
                for cb in range(tq // LANES):
                    part = blk[:, cb * LANES:(cb + 1) * LANES]
                    mx = part if mx is None else jnp.maximum(mx, part)
            ct = jnp.transpose(jnp.broadcast_to(crow_ref[0, hh, c:c + 1, :], (SUBLANES, tq)))[:, 0:1]
            m = jnp.max(mx, axis=-1, keepdims=True) + ct
            shift.append(m - ct)
        outs = []
        for (c, hh), sj, sh in zip(probs, s, shift):
            acc = None
            for j in range(c + 1):
                v = rows(v_ref, j)
                v_aug = jnp.concatenate([v, jnp.ones_like(v)], axis=1)
                d = jnp.dot(jnp.exp2(sj[j] - sh).astype(BF16), v_aug, preferred_element_type=F32)
                acc = d if acc is None else acc + d
            outs.append(acc[:, :LANES] / acc[:, LANES:])
        for n, c in enumerate((first, nblk - 1 - first)):
            o_ref[0, c * tq:(c + 1) * tq, :] = jnp.where(
                lane < FOX_HEAD_DIM, outs[2 * n], outs[2 * n + 1]).astype(o_ref.dtype)


def _fox_attn(qkv, c_row, *, tq):
    bsz, seq, _ = qkv.shape
    n_pairs = FOX_WIDTH // LANES
    nblk = seq // tq
    seq_block = lambda col0: pl.BlockSpec((1, seq, LANES), lambda b, p: (b, 0, col0 + p))
    return pl.pallas_call(
        functools.partial(_fox_attn_kernel, tq=tq),
        grid=(bsz, n_pairs),
        in_specs=[
            seq_block(0),
            seq_block(n_pairs),
            seq_block(2 * n_pairs),
            pl.BlockSpec((1, 2, nblk, tq), lambda b, p: (b, p, 0, 0)),
        ],
        out_specs=seq_block(0),
        out_shape=jax.ShapeDtypeStruct((bsz, seq, FOX_WIDTH), BF16),
        compiler_params=_params("parallel", "parallel"),
        name="fox_attn",
    )(qkv, qkv, qkv, c_row)


def _post_kernel(*refs, n_parts, final):
    parts = refs[:n_parts]
    x_ref, wo_ref, g_ref, w1_ref, w2_ref, gf_ref, o_ref = refs[n_parts:]
    hidden = w2_ref.shape[0]
    mix = None
    off = 0
    for p_ref in parts:
        width = p_ref.shape[1]
        d = jnp.dot(p_ref[...], wo_ref[off:off + width, :], preferred_element_type=F32)
        mix = d if mix is None else mix + d
        off += width
    x1 = x_ref[...] + mix
    h = _rms_norm(x1, g_ref[...]).astype(BF16)
    gu = jnp.dot(h, w1_ref[...], preferred_element_type=F32)
    gate = gu[:, :hidden]
    act = (gate * _sigmoid(gate) * gu[:, hidden:]).astype(BF16)
    x2 = x1 + jnp.dot(act, w2_ref[...], preferred_element_type=F32)
    if final:
        x2 = _rms_norm(x2, gf_ref[...])
    o_ref[...] = x2


def _post(parts, x2d, wo, g, w1, w2, gf, *, final, tm=512):
    t = x2d.shape[0]
    kern = functools.partial(_post_kernel, n_parts=len(parts), final=final)
    in_specs = [pl.BlockSpec((tm, p.shape[1]), lambda i: (i, 0)) for p in parts]
    in_specs += [
        pl.BlockSpec((tm, D_MODEL), lambda i: (i, 0)),
        _resident(wo.shape),
        _resident((1, D_MODEL)),
        _resident(w1.shape),
        _resident(w2.shape),
        _resident((1, D_MODEL)),
    ]
    return pl.pallas_call(
        kern,
        grid=(t // tm,),
        in_specs=in_specs,
        out_specs=pl.BlockSpec((tm, D_MODEL), lambda i: (i, 0)),
        out_shape=jax.ShapeDtypeStruct((t, D_MODEL), F32),
        compiler_params=_params("parallel"),
        name="post",
    )(*parts, x2d, wo, g, w1, w2, gf)


def _hgrn_levels():
    h, out = HGRN_CHUNK // 2, []
    while h >= 1:
        out.append(h)
        h //= 2
    return out


def _hgrn_cum_table():
    t = np.arange(HGRN_CHUNK)
    m = (t[None, :] <= t[:, None]).astype(np.float32)
    return np.concatenate([m, m, m], axis=1)


def _hgrn_masks():
    c = HGRN_CHUNK
    levels = _hgrn_levels()
    row = lax.broadcasted_iota(jnp.int32, (c, HGRN_HEAD_DIM), 0)
    r2 = lax.broadcasted_iota(jnp.int32, (c, c), 0)
    c2 = lax.broadcasted_iota(jnp.int32, (c, c), 1)
    upper = [(row & h) != 0 for h in levels]
    pair = [((r2 // (2 * h)) == (c2 // (2 * h))) & ((r2 & h) != 0) & ((c2 & h) == 0)
            for h in levels]
    return row, upper, pair


def _hgrn_block(q16, f_logit, v, g16, lb, hn, cum, st, masks):
    c = HGRN_CHUNK
    levels = _hgrn_levels()
    row, upper, pair = masks

    n = q16.shape[0] // c
    chunk = lambda arr, i: arr[i * c:(i + 1) * c]
    q = q16.astype(F32)
    f = lb + (1.0 - lb) * _sigmoid(f_logit)
    k = 1.0 - f
    lg = jnp.log2(f)
    hi = lg.astype(BF16)
    r1 = lg - hi.astype(F32)
    mid = r1.astype(BF16)
    lo = (r1 - mid.astype(F32)).astype(BF16)
    b = [jnp.dot(cum, jnp.concatenate([chunk(hi, i), chunk(mid, i), chunk(lo, i)], axis=0),
                 preferred_element_type=F32) for i in range(n)]

    a = [None] * n
    for li, h in enumerate(levels):
        xs = []
        for i in range(n):
            qi, ki, bi = chunk(q, i), chunk(k, i), b[i]
            if h == 1:
                x = jnp.where(upper[li], qi * chunk(f, i), ki)
            else:
                if h >= SUBLANES:
                    ref = jnp.concatenate(
                        [jnp.broadcast_to(bi[base + h - 1:base + h, :], (2 * h, HGRN_HEAD_DIM))
                         for base in range(0, c, 2 * h)], axis=0)
                else:
                    b3 = bi.reshape(c // SUBLANES, SUBLANES, HGRN_HEAD_DIM)
                    refs = [jnp.broadcast_to(b3[:, base + h - 1:base + h, :], b3.shape).reshape(bi.shape)
                            for base in range(0, SUBLANES, 2 * h)]
                    ref = refs[0]
                    for m in range(1, len(refs)):
                        ref = jnp.where((row % SUBLANES) >= m * 2 * h, refs[m], ref)
                x = jnp.where(upper[li], qi, ki) * jnp.exp2(-jnp.abs(bi - ref))
            xs.append(x.astype(BF16))
        ps = [lax.dot_general(x, x, NT_DIMS, preferred_element_type=F32) for x in xs]
        a = [jnp.where(pair[li], ps[i], 0.0 if a[i] is None else a[i]) for i in range(n)]
    o_intra = [jnp.dot(a[i].astype(BF16), chunk(v, i), preferred_element_type=F32) for i in range(n)]

    b_last = [b[i][c - 1:c, :] for i in range(n)]
    kv = [lax.dot_general(chunk(v, i), (chunk(k, i) * jnp.exp2(b_last[i] - b[i])).astype(BF16),
                          TN_DIMS, preferred_element_type=F32) for i in range(n)]
    o_inter = []
    for i in range(n):
        o_inter.append(lax.dot_general((chunk(q, i) * jnp.exp2(b[i])).astype(BF16), st.astype(BF16),
                                       NT_DIMS, preferred_element_type=F32))
        st = st * jnp.exp2(b_last[i]) + kv[i]

    o = jnp.concatenate([o_inter[i] + o_intra[i] for i in range(n)], axis=0)
    o = o + jnp.sum(q * k, axis=-1, keepdims=True) * v.astype(F32)
    o = o * lax.rsqrt(jnp.mean(o * o, axis=-1, keepdims=True) + EPS) * hn
    g = g16.astype(F32)
    return (o * (g * _sigmoid(g))).astype(BF16), st


def _c_in_kernel(x_ref, g_ref, w_ref, o_ref, f_ref):
    h = _rms_norm(x_ref[...], g_ref[...]).astype(BF16)
    pr = jnp.dot(h, w_ref[...], preferred_element_type=F32)
    o_ref[:, :D_MODEL] = pr[:, :D_MODEL].astype(BF16)
    o_ref[:, D_MODEL:] = pr[:, 2 * D_MODEL:].astype(BF16)
    f_ref[...] = pr[:, D_MODEL:2 * D_MODEL]


def _c_in(x2d, g, w, *, tm=512):
    t = x2d.shape[0]
    return pl.pallas_call(
        _c_in_kernel,
        grid=(t // tm,),
        in_specs=[
            pl.BlockSpec((tm, D_MODEL), lambda i: (i, 0)),
            _resident((1, D_MODEL)),
            _resident(w.shape),
        ],
        out_specs=[
            pl.BlockSpec((tm, 3 * D_MODEL), lambda i: (i, 0)),
            pl.BlockSpec((tm, D_MODEL), lambda i: (i, 0)),
        ],
        out_shape=[
            jax.ShapeDtypeStruct((t, 3 * D_MODEL), BF16),
            jax.ShapeDtypeStruct((t, D_MODEL), F32),
        ],
        compiler_params=_params("parallel"),
        name="c_in",
    )(x2d, g, w)


def _hgrn_kernel(q_ref, f_ref, i_ref, g_ref, lbraw_ref, hn_ref, cum_ref, o_ref, state, *,
                 layer_idx):
    @pl.when(pl.program_id(2) == 0)
    def _():
        state[...] = jnp.zeros_like(state)

    raw = lbraw_ref[...]
    e = jnp.exp(raw - jnp.max(raw, axis=0, keepdims=True))
    sm = e / jnp.sum(e, axis=0, keepdims=True)
    lb = jnp.sum(sm[0:layer_idx + 1, :], axis=0, keepdims=True) - sm[0:1, :]
    o, st = _hgrn_block(q_ref[0], f_ref[0], i_ref[0], g_ref[0], lb, hn_ref[...], cum_ref[...],
                        state[...], _hgrn_masks())
    state[...] = st
    o_ref[0] = o


def _hgrn(proj, f_logit, lb_raw, head_norm, *, layer_idx, blk=2048):
    bsz, seq, _ = f_logit.shape
    hd = HGRN_HEAD_DIM
    nh = HGRN_HEADS
    cum = jnp.asarray(_hgrn_cum_table(), BF16)
    return pl.pallas_call(
        functools.partial(_hgrn_kernel, layer_idx=layer_idx),
        grid=(bsz, nh, seq // blk),
        in_specs=[
            pl.BlockSpec((1, blk, hd), lambda b, h, l: (b, l, h)),
            pl.BlockSpec((1, blk, hd), lambda b, h, l: (b, l, h)),
            pl.BlockSpec((1, blk, hd), lambda b, h, l: (b, l, nh + h)),
            pl.BlockSpec((1, blk, hd), lambda b, h, l: (b, l, 2 * nh + h)),
            pl.BlockSpec((lb_raw.shape[0], hd), lambda b, h, l: (0, h)),
            pl.BlockSpec((1, hd), lambda b, h, l: (0, h)),
            _resident(cum.shape),
        ],
        out_specs=pl.BlockSpec((1, blk, hd), lambda b, h, l: (b, l, h)),
        out_shape=jax.ShapeDtypeStruct((bsz, seq, D_MODEL), BF16),
        scratch_shapes=[pltpu.VMEM((hd, hd), F32)],
        compiler_params=_params("parallel", "parallel", "arbitrary"),
        name="hgrn",
    )(proj, f_logit, proj, proj, lb_raw, head_norm, cum)


def kernel(x, norm_mix, norm_ffn, final_norm, ab_w_in, fox_f_bias, conv_w, ab_w_out,
           c_w_in, c_lower_bounds, c_head_norm, c_w_out, ffn_w_in, ffn_w_out):
    bsz, seq, d = x.shape
    depth = norm_mix.shape[0]
    t = bsz * seq
    x2d = x.reshape(t, d)
    gf = final_norm.reshape(1, d)
    q_end = 3 * FOX_WIDTH
    f_end = q_end + FOX_HEADS

    for layer in range(depth):
        j = layer // 2
        g_mix = norm_mix[layer].reshape(1, d)
        if layer % 2 == 0:
            w_in = ab_w_in[j]
            wqkv = w_in[:, :q_end].astype(BF16)
            wf = jnp.pad(w_in[:, q_end:f_end], ((0, 0), (0, LANES - FOX_HEADS))).astype(BF16)
            wconv = w_in[:, f_end:].astype(BF16)
            qkv, f_pad, b_out = _ab_in(x2d, g_mix, wqkv, wf, wconv, conv_w[j], seq=seq)
            c = _fox_cumsum(f_pad.reshape(bsz, seq, LANES), fox_f_bias[j].reshape(FOX_HEADS, 1))
            tq = FOX_Q_TILE
            c_row = c.reshape(bsz, FOX_HEADS, seq // tq, tq)
            a_out = _fox_attn(qkv.reshape(bsz, seq, q_end), c_row, tq=tq)
            parts = [a_out.reshape(t, FOX_WIDTH), b_out]
            w_out = ab_w_out[j]
        else:
            proj, f_logit = _c_in(x2d, g_mix, c_w_in[j].astype(BF16))
            o = _hgrn(proj.reshape(bsz, seq, 3 * d), f_logit.reshape(bsz, seq, d),
                      c_lower_bounds, c_head_norm[j].reshape(1, d), layer_idx=j)
            parts = [o.reshape(t, d)]
            w_out = c_w_out[j]
        x2d = _post(parts, x2d, w_out.astype(BF16), norm_ffn[layer].reshape(1, d),
                    ffn_w_in[layer].astype(BF16), ffn_w_out[layer].astype(BF16), gf,
                    final=(layer == depth - 1))
    return x2d.reshape(bsz, seq, d)
```

```python
import functools

import jax
import jax.numpy as jnp
import numpy as np
from jax import lax
from jax.experimental import pallas as pl
from jax.experimental.pallas import tpu as pltpu

D_MODEL = 1024
EPS = 1e-6
NEG_BIG = -1e30

FOX_HEADS = 8
FOX_HEAD_DIM = 64
FOX_WIDTH = FOX_HEADS * FOX_HEAD_DIM
CONV_WIDTH = D_MODEL - FOX_WIDTH
CONV_TAPS = 3
LOG2E = 1.4426950408889634
FOX_Q_SCALE = LOG2E * FOX_HEAD_DIM ** -0.5
FOX_Q_TILE = 256
HGRN_HEADS = 8
HGRN_HEAD_DIM = D_MODEL // HGRN_HEADS
HGRN_CHUNK = 64

LANES = 128
SUBLANES = 8
VMEM_LIMIT = 56 * 1024 * 1024

BF16 = jnp.bfloat16
F32 = jnp.float32

NT_DIMS = (((1,), (1,)), ((), ()))
TN_DIMS = (((0,), (0,)), ((), ()))


def _resident(shape):
    zeros = (0,) * len(shape)
    return pl.BlockSpec(shape, lambda *_: zeros, pipeline_mode=pl.Buffered(1))


def _params(*sem):
    return pltpu.CompilerParams(dimension_semantics=sem, vmem_limit_bytes=VMEM_LIMIT)


def _rms_norm(x, g):
    return x * lax.rsqrt(jnp.mean(x * x, axis=-1, keepdims=True) + EPS) * g


def _sigmoid(x):
    return 0.5 * jnp.tanh(0.5 * x) + 0.5


def _cast_operands(jobs, n_steps, step_of):
    ins, in_specs, out_specs, out_shapes = [], [], [], []
    for w, col_ranges in jobs:
        rows, cols = w.shape
        slab = rows // n_steps
        assert slab * n_steps == rows
        index_map = lambda *g: (step_of(*g), 0, 0)
        ins.append(w.reshape(n_steps, slab, cols))
        in_specs.append(pl.BlockSpec((1, slab, cols), index_map))
        for c0, c1 in col_ranges:
            out_specs.append(pl.BlockSpec((1, slab, c1 - c0), index_map))
            out_shapes.append(jax.ShapeDtypeStruct((n_steps, slab, c1 - c0), BF16))
    return ins, in_specs, out_specs, out_shapes


def _cast_slabs(jobs_cols, in_refs, out_refs):
    outs = iter(out_refs)
    for ref, col_ranges in zip(in_refs, jobs_cols):
        for c0, c1 in col_ranges:
            next(outs)[0] = ref[0, :, c0:c1].astype(BF16)


def _cast_results(jobs, outs):
    outs = iter(outs)
    return [[next(outs).reshape(w.shape[0], c1 - c0) for c0, c1 in col_ranges]
            for w, col_ranges in jobs]


def _ab_in_kernel(x_ref, g_ref, wqkv_ref, wf_ref, wconv_ref, cw_ref,
                  qkv_ref, f_ref, b_ref, zbuf, *, tm, seq):
    i = pl.program_id(0)

    @pl.when((i * tm) % seq == 0)
    def _():
        zbuf[0:SUBLANES, :] = jnp.zeros((SUBLANES, CONV_WIDTH), F32)

    h = _rms_norm(x_ref[...], g_ref[...]).astype(BF16)
    u = jnp.dot(h, wconv_ref[...], preferred_element_type=F32)
    u_b = u[:, :CONV_WIDTH]
    z = u[:, CONV_WIDTH:2 * CONV_WIDTH] * u[:, 2 * CONV_WIDTH:]
    zbuf[SUBLANES:SUBLANES + tm, :] = z
    cw = cw_ref[...]
    y = (cw[0:1, :] * zbuf[SUBLANES - 2:SUBLANES - 2 + tm, :]
         + cw[1:2, :] * zbuf[SUBLANES - 1:SUBLANES - 1 + tm, :]
         + cw[2:3, :] * z)
    b_ref[...] = (u_b * y).astype(BF16)
    zbuf[0:SUBLANES, :] = zbuf[tm:tm + SUBLANES, :]

    f_ref[...] = jnp.dot(h, wf_ref[...], preferred_element_type=F32)
    qkv = jnp.dot(h, wqkv_ref[...], preferred_element_type=F32)
    qkv_ref[:, :FOX_WIDTH] = (qkv[:, :FOX_WIDTH] * FOX_Q_SCALE).astype(BF16)
    qkv_ref[:, FOX_WIDTH:] = qkv[:, FOX_WIDTH:].astype(BF16)


def _ab_in(x2d, g, wqkv, wf, wconv, cw, *, seq, tm=512):
    t = x2d.shape[0]
    kern = functools.partial(_ab_in_kernel, tm=tm, seq=seq)
    return pl.pallas_call(
        kern,
        grid=(t // tm,),
        in_specs=[
            pl.BlockSpec((tm, D_MODEL), lambda i: (i, 0)),
            _resident((1, D_MODEL)),
            _resident(wqkv.shape),
            _resident(wf.shape),
            _resident(wconv.shape),
            _resident(cw.shape),
        ],
        out_specs=[
            pl.BlockSpec((tm, 3 * FOX_WIDTH), lambda i: (i, 0)),
            pl.BlockSpec((tm, LANES), lambda i: (i, 0)),
            pl.BlockSpec((tm, CONV_WIDTH), lambda i: (i, 0)),
        ],
        out_shape=[
            jax.ShapeDtypeStruct((t, 3 * FOX_WIDTH), BF16),
            jax.ShapeDtypeStruct((t, LANES), F32),
            jax.ShapeDtypeStruct((t, CONV_WIDTH), BF16),
        ],
        scratch_shapes=[pltpu.VMEM((tm + SUBLANES, CONV_WIDTH), F32)],
        compiler_params=_params("arbitrary"),
        name="ab_in",
    )(x2d, g, wqkv, wf, wconv, cw)


def _fox_cumsum_kernel(f_ref, bias_ref, c_ref):
    heads, seq = c_ref.shape[1], c_ref.shape[2]
    lane = lax.broadcasted_iota(jnp.int32, (heads, LANES), 1)
    carry = jnp.zeros((heads, 1), F32)
    for j in range(seq // LANES):
        x = jnp.transpose(f_ref[0, j * LANES:(j + 1) * LANES, :])[0:heads, :] + bias_ref[...]
        blk = jnp.minimum(x, 0.0) - jnp.log(1.0 + jnp.exp(-jnp.abs(x)))
        shift = 1
        while shift < LANES:
            blk = blk + jnp.where(lane >= shift, pltpu.roll(blk, shift, 1), 0.0)
            shift *= 2
        blk = blk + carry
        c_ref[0, :, j * LANES:(j + 1) * LANES] = blk * LOG2E
        carry = blk[:, LANES - 1:LANES]


def _fox_cumsum(f_tok, bias_col):
    bsz, seq, _ = f_tok.shape
    heads = bias_col.shape[0]
    return pl.pallas_call(
        _fox_cumsum_kernel,
        grid=(bsz,),
        in_specs=[pl.BlockSpec((1, seq, LANES), lambda b: (b, 0, 0)), _resident(bias_col.shape)],
        out_specs=pl.BlockSpec((1, heads, seq), lambda b: (b, 0, 0)),
        out_shape=jax.ShapeDtypeStruct((bsz, heads, seq), F32),
        compiler_params=_params("parallel"),
        name="fox_cumsum",
    )(f_tok, bias_col)


def _fox_attn_kernel(q_ref, k_ref, v_ref, crow_ref, *rest, tq, cast_cols):
    cast_in, o_ref, cast_out = rest[:len(cast_cols)], rest[len(cast_cols)], rest[len(cast_cols) + 1:]
    _cast_slabs(cast_cols, cast_in, cast_out)
    nblk = k_ref.shape[1] // tq
    lane = lax.broadcasted_iota(jnp.int32, (1, LANES), 1)
    row = lax.broadcasted_iota(jnp.int32, (tq, tq), 0)
    col = lax.broadcasted_iota(jnp.int32, (tq, tq), 1)
    causal = row >= col
    in_head = [(lane >= hh * FOX_HEAD_DIM) & (lane < (hh + 1) * FOX_HEAD_DIM) for hh in range(2)]
    rows = lambda ref, c: ref[0, c * tq:(c + 1) * tq, :]

    for first in range(nblk // 2):
        probs = [(c, hh) for c in (first, nblk - 1 - first) for hh in range(2)]
        s = []
        for c, hh in probs:
            q = rows(q_ref, c)
            qh = jnp.where(in_head[hh], q, jnp.zeros_like(q))
            sj = []
            for j in range(c + 1):
                sh = lax.dot_general(qh, rows(k_ref, j), NT_DIMS, preferred_element_type=F32)
                sh = sh - crow_ref[0, hh, j:j + 1, :]
                sj.append(jnp.where(causal, sh, NEG_BIG) if j == c else sh)
            s.append(sj)
        shift = []
        for (c, hh), sj in zip(probs, s):
            mx = None
            for blk in sj:
                for cb in range(tq // LANES):
                    part = blk[:, cb * LANES:(cb + 1) * LANES]
                    mx = part if mx is None else jnp.maximum(mx, part)
            ct = jnp.transpose(jnp.broadcast_to(crow_ref[0, hh, c:c + 1, :], (SUBLANES, tq)))[:, 0:1]
            m = jnp.max(mx, axis=-1, keepdims=True) + ct
            shift.append(m - ct)
        outs = []
        for (c, hh), sj, sh in zip(probs, s, shift):
            acc = None
            for j in range(c + 1):
                v = rows(v_ref, j)
                v_aug = jnp.concatenate([v, jnp.ones_like(v)], axis=1)
                d = jnp.dot(jnp.exp2(sj[j] - sh).astype(BF16), v_aug, preferred_element_type=F32)
                acc = d if acc is None else acc + d
            outs.append(acc[:, :LANES] / acc[:, LANES:])
        for n, c in enumerate((first, nblk - 1 - first)):
            o_ref[0, c * tq:(c + 1) * tq, :] = jnp.where(
                lane < FOX_HEAD_DIM, outs[2 * n], outs[2 * n + 1]).astype(o_ref.dtype)


def _fox_attn(qkv, c_row, cast_jobs, *, tq):
    bsz, seq, _ = qkv.shape
    n_pairs = FOX_WIDTH // LANES
    nblk = seq // tq
    seq_block = lambda col0: pl.BlockSpec((1, seq, LANES), lambda b, p: (b, 0, col0 + p))
    c_ins, c_in_specs, c_out_specs, c_out_shapes = _cast_operands(
        cast_jobs, bsz * n_pairs, lambda b, p: b * n_pairs + p)
    outs = pl.pallas_call(
        functools.partial(_fox_attn_kernel, tq=tq, cast_cols=[cols for _, cols in cast_jobs]),
        grid=(bsz, n_pairs),
        in_specs=[
            seq_block(0),
            seq_block(n_pairs),
            seq_block(2 * n_pairs),
            pl.BlockSpec((1, 2, nblk, tq), lambda b, p: (b, p, 0, 0)),
        ] + c_in_specs,
        out_specs=[seq_block(0)] + c_out_specs,
        out_shape=[jax.ShapeDtypeStruct((bsz, seq, FOX_WIDTH), BF16)] + c_out_shapes,
        compiler_params=_params("parallel", "parallel"),
        name="fox_attn",
    )(qkv, qkv, qkv, c_row, *c_ins)
    return outs[0], _cast_results(cast_jobs, outs[1:])


def _post_kernel(*refs, n_parts, final):
    parts = refs[:n_parts]
    x_ref, wo_ref, g_ref, w1_ref, w2_ref, gf_ref, o_ref = refs[n_parts:]
    hidden = w2_ref.shape[0]
    mix = None
    off = 0
    for p_ref in parts:
        width = p_ref.shape[1]
        d = jnp.dot(p_ref[...], wo_ref[off:off + width, :], preferred_element_type=F32)
        mix = d if mix is None else mix + d
        off += width
    x1 = x_ref[...] + mix
    h = _rms_norm(x1, g_ref[...]).astype(BF16)
    gu = jnp.dot(h, w1_ref[...], preferred_element_type=F32)
    gate = gu[:, :hidden]
    act = (gate * _sigmoid(gate) * gu[:, hidden:]).astype(BF16)
    x2 = x1 + jnp.dot(act, w2_ref[...], preferred_element_type=F32)
    if final:
        x2 = _rms_norm(x2, gf_ref[...])
    o_ref[...] = x2


def _post(parts, x2d, wo, g, w1, w2, gf, *, final, tm=512):
    t = x2d.shape[0]
    kern = functools.partial(_post_kernel, n_parts=len(parts), final=final)
    in_specs = [pl.BlockSpec((tm, p.shape[1]), lambda i: (i, 0)) for p in parts]
    in_specs += [
        pl.BlockSpec((tm, D_MODEL), lambda i: (i, 0)),
        _resident(wo.shape),
        _resident((1, D_MODEL)),
        _resident(w1.shape),
        _resident(w2.shape),
        _resident((1, D_MODEL)),
    ]
    return pl.pallas_call(
        kern,
        grid=(t // tm,),
        in_specs=in_specs,
        out_specs=pl.BlockSpec((tm, D_MODEL), lambda i: (i, 0)),
        out_shape=jax.ShapeDtypeStruct((t, D_MODEL), F32),
        compiler_params=_params("parallel"),
        name="post",
    )(*parts, x2d, wo, g, w1, w2, gf)


def _hgrn_levels():
    h, out = HGRN_CHUNK // 2, []
    while h >= 1:
        out.append(h)
        h //= 2
    return out


def _hgrn_cum_table():
    t = np.arange(HGRN_CHUNK)
    m = (t[None, :] <= t[:, None]).astype(np.float32)
    return np.concatenate([m, m, m], axis=1)


def _hgrn_masks():
    c = HGRN_CHUNK
    levels = _hgrn_levels()
    row = lax.broadcasted_iota(jnp.int32, (c, HGRN_HEAD_DIM), 0)
    r2 = lax.broadcasted_iota(jnp.int32, (c, c), 0)
    c2 = lax.broadcasted_iota(jnp.int32, (c, c), 1)
    upper = [(row & h) != 0 for h in levels]
    pair = [((r2 // (2 * h)) == (c2 // (2 * h))) & ((r2 & h) != 0) & ((c2 & h) == 0)
            for h in levels]
    return row, upper, pair


def _hgrn_block(q16, f_logit, v, g16, lb, hn, cum, st, masks):
    c = HGRN_CHUNK
    levels = _hgrn_levels()
    row, upper, pair = masks

    n = q16.shape[0] // c
    chunk = lambda arr, i: arr[i * c:(i + 1) * c]
    q = q16.astype(F32)
    f = lb + (1.0 - lb) * _sigmoid(f_logit)
    k = 1.0 - f
    lg = jnp.log2(f)
    hi = lg.astype(BF16)
    r1 = lg - hi.astype(F32)
    mid = r1.astype(BF16)
    lo = (r1 - mid.astype(F32)).astype(BF16)
    b = [jnp.dot(cum, jnp.concatenate([chunk(hi, i), chunk(mid, i), chunk(lo, i)], axis=0),
                 preferred_element_type=F32) for i in range(n)]

    a = [None] * n
    for li, h in enumerate(levels):
        xs = []
        for i in range(n):
            qi, ki, bi = chunk(q, i), chunk(k, i), b[i]
            if h == 1:
                x = jnp.where(upper[li], qi * chunk(f, i), ki)
            else:
                if h >= SUBLANES:
                    ref = jnp.concatenate(
                        [jnp.broadcast_to(bi[base + h - 1:base + h, :], (2 * h, HGRN_HEAD_DIM))
                         for base in range(0, c, 2 * h)], axis=0)
                else:
                    b3 = bi.reshape(c // SUBLANES, SUBLANES, HGRN_HEAD_DIM)
                    refs = [jnp.broadcast_to(b3[:, base + h - 1:base + h, :], b3.shape).reshape(bi.shape)
                            for base in range(0, SUBLANES, 2 * h)]
                    ref = refs[0]
                    for m in range(1, len(refs)):
                        ref = jnp.where((row % SUBLANES) >= m * 2 * h, refs[m], ref)
                x = jnp.where(upper[li], qi, ki) * jnp.exp2(-jnp.abs(bi - ref))
            xs.append(x.astype(BF16))
        ps = [lax.dot_general(x, x, NT_DIMS, preferred_element_type=F32) for x in xs]
        a = [jnp.where(pair[li], ps[i], 0.0 if a[i] is None else a[i]) for i in range(n)]
    o_intra = [jnp.dot(a[i].astype(BF16), chunk(v, i), preferred_element_type=F32) for i in range(n)]

    b_last = [b[i][c - 1:c, :] for i in range(n)]
    kv = [lax.dot_general(chunk(v, i), (chunk(k, i) * jnp.exp2(b_last[i] - b[i])).astype(BF16),
                          TN_DIMS, preferred_element_type=F32) for i in range(n)]
    o_inter = []
    for i in range(n):
        o_inter.append(lax.dot_general((chunk(q, i) * jnp.exp2(b[i])).astype(BF16), st.astype(BF16),
                                       NT_DIMS, preferred_element_type=F32))
        st = st * jnp.exp2(b_last[i]) + kv[i]

    o = jnp.concatenate([o_inter[i] + o_intra[i] for i in range(n)], axis=0)
    o = o + jnp.sum(q * k, axis=-1, keepdims=True) * v.astype(F32)
    o = o * lax.rsqrt(jnp.mean(o * o, axis=-1, keepdims=True) + EPS) * hn
    g = g16.astype(F32)
    return (o * (g * _sigmoid(g))).astype(BF16), st


def _c_in_kernel(x_ref, g_ref, w_ref, o_ref, f_ref):
    h = _rms_norm(x_ref[...], g_ref[...]).astype(BF16)
    pr = jnp.dot(h, w_ref[...], preferred_element_type=F32)
    o_ref[:, :D_MODEL] = pr[:, :D_MODEL].astype(BF16)
    o_ref[:, D_MODEL:] = pr[:, 2 * D_MODEL:].astype(BF16)
    f_ref[...] = pr[:, D_MODEL:2 * D_MODEL]


def _c_in(x2d, g, w, *, tm=512):
    t = x2d.shape[0]
    return pl.pallas_call(
        _c_in_kernel,
        grid=(t // tm,),
        in_specs=[
            pl.BlockSpec((tm, D_MODEL), lambda i: (i, 0)),
            _resident((1, D_MODEL)),
            _resident(w.shape),
        ],
        out_specs=[
            pl.BlockSpec((tm, 3 * D_MODEL), lambda i: (i, 0)),
            pl.BlockSpec((tm, D_MODEL), lambda i: (i, 0)),
        ],
        out_shape=[
            jax.ShapeDtypeStruct((t, 3 * D_MODEL), BF16),
            jax.ShapeDtypeStruct((t, D_MODEL), F32),
        ],
        compiler_params=_params("parallel"),
        name="c_in",
    )(x2d, g, w)


def _hgrn_kernel(q_ref, f_ref, i_ref, g_ref, lbraw_ref, hn_ref, cum_ref, *rest,
                 layer_idx, cast_cols):
    n_cast = len(cast_cols)
    cast_in, o_ref, cast_out, state = rest[:n_cast], rest[n_cast], rest[n_cast + 1:-1], rest[-1]
    _cast_slabs(cast_cols, cast_in, cast_out)

    @pl.when(pl.program_id(2) == 0)
    def _():
        state[...] = jnp.zeros_like(state)

    raw = lbraw_ref[...]
    e = jnp.exp(raw - jnp.max(raw, axis=0, keepdims=True))
    sm = e / jnp.sum(e, axis=0, keepdims=True)
    lb = jnp.sum(sm[0:layer_idx + 1, :], axis=0, keepdims=True) - sm[0:1, :]
    o, st = _hgrn_block(q_ref[0], f_ref[0], i_ref[0], g_ref[0], lb, hn_ref[...], cum_ref[...],
                        state[...], _hgrn_masks())
    state[...] = st
    o_ref[0] = o


def _hgrn(proj, f_logit, lb_raw, head_norm, cast_jobs, *, layer_idx, blk=2048):
    bsz, seq, _ = f_logit.shape
    hd = HGRN_HEAD_DIM
    nh = HGRN_HEADS
    n_l = seq // blk
    cum = jnp.asarray(_hgrn_cum_table(), BF16)
    c_ins, c_in_specs, c_out_specs, c_out_shapes = _cast_operands(
        cast_jobs, bsz * nh * n_l, lambda b, h, l: (b * nh + h) * n_l + l)
    outs = pl.pallas_call(
        functools.partial(_hgrn_kernel, layer_idx=layer_idx,
                          cast_cols=[cols for _, cols in cast_jobs]),
        grid=(bsz, nh, n_l),
        in_specs=[
            pl.BlockSpec((1, blk, hd), lambda b, h, l: (b, l, h)),
            pl.BlockSpec((1, blk, hd), lambda b, h, l: (b, l, h)),
            pl.BlockSpec((1, blk, hd), lambda b, h, l: (b, l, nh + h)),
            pl.BlockSpec((1, blk, hd), lambda b, h, l: (b, l, 2 * nh + h)),
            pl.BlockSpec((lb_raw.shape[0], hd), lambda b, h, l: (0, h)),
            pl.BlockSpec((1, hd), lambda b, h, l: (0, h)),
            _resident(cum.shape),
        ] + c_in_specs,
        out_specs=[pl.BlockSpec((1, blk, hd), lambda b, h, l: (b, l, h))] + c_out_specs,
        out_shape=[jax.ShapeDtypeStruct((bsz, seq, D_MODEL), BF16)] + c_out_shapes,
        scratch_shapes=[pltpu.VMEM((hd, hd), F32)],
        compiler_params=_params("parallel", "parallel", "arbitrary"),
        name="hgrn",
    )(proj, f_logit, proj, proj, lb_raw, head_norm, cum, *c_ins)
    return outs[0], _cast_results(cast_jobs, outs[1:])


def kernel(x, norm_mix, norm_ffn, final_norm, ab_w_in, fox_f_bias, conv_w, ab_w_out,
           c_w_in, c_lower_bounds, c_head_norm, c_w_out, ffn_w_in, ffn_w_out):
    bsz, seq, d = x.shape
    depth = norm_mix.shape[0]
    t = bsz * seq
    x2d = x.reshape(t, d)
    gf = final_norm.reshape(1, d)
    q_end = 3 * FOX_WIDTH
    f_end = q_end + FOX_HEADS

    ab_cols = [(0, q_end), (q_end, q_end + LANES), (f_end, f_end + 3 * CONV_WIDTH)]
    whole = lambda w: [(0, w.shape[1])]

    def in_proj_job(layer):
        j = layer // 2
        return (ab_w_in[j], ab_cols) if layer % 2 == 0 else (c_w_in[j], whole(c_w_in[j]))

    w0, cols0 = in_proj_job(0)
    in_w = [w0[:, c0:c1].astype(BF16) for c0, c1 in cols0]

    for layer in range(depth):
        j = layer // 2
        g_mix = norm_mix[layer].reshape(1, d)
        w_out = ab_w_out[j] if layer % 2 == 0 else c_w_out[j]
        jobs = [(w_out, whole(w_out)), (ffn_w_in[layer], whole(ffn_w_in[layer])),
                (ffn_w_out[layer], whole(ffn_w_out[layer]))]
        if layer + 1 < depth:
            jobs.append(in_proj_job(layer + 1))
        if layer % 2 == 0:
            wqkv, wf, wconv = in_w
            qkv, f_pad, b_out = _ab_in(x2d, g_mix, wqkv, wf, wconv, conv_w[j], seq=seq)
            c = _fox_cumsum(f_pad.reshape(bsz, seq, LANES), fox_f_bias[j].reshape(FOX_HEADS, 1))
            tq = FOX_Q_TILE
            c_row = c.reshape(bsz, FOX_HEADS, seq // tq, tq)
            a_out, cast = _fox_attn(qkv.reshape(bsz, seq, q_end), c_row, jobs, tq=tq)
            parts = [a_out.reshape(t, FOX_WIDTH), b_out]
        else:
            proj, f_logit = _c_in(x2d, g_mix, in_w[0])
            o, cast = _hgrn(proj.reshape(bsz, seq, 3 * d), f_logit.reshape(bsz, seq, d),
                            c_lower_bounds, c_head_norm[j].reshape(1, d), jobs, layer_idx=j)
            parts = [o.reshape(t, d)]
        (wo,), (w1,), (w2,) = cast[:3]
        in_w = cast[3] if layer + 1 < depth else None
        x2d = _post(parts, x2d, wo, norm_ffn[layer].reshape(1, d), w1, w2, gf,
                    final=(layer == depth - 1))
    return x2d.reshape(bsz, seq, d)
```

```python
import functools

import jax
import jax.numpy as jnp
import numpy as np
from jax import lax
from jax.experimental import pallas as pl
from jax.experimental.pallas import tpu as pltpu

D_MODEL = 1024
EPS = 1e-6
NEG_BIG = -1e30

FOX_HEADS = 8
FOX_HEAD_DIM = 64
FOX_WIDTH = FOX_HEADS * FOX_HEAD_DIM
CONV_WIDTH = D_MODEL - FOX_WIDTH
CONV_TAPS = 3
LOG2E = 1.4426950408889634
FOX_Q_SCALE = LOG2E * FOX_HEAD_DIM ** -0.5
FOX_Q_TILE = 256
HGRN_HEADS = 8
HGRN_HEAD_DIM = D_MODEL // HGRN_HEADS
HGRN_CHUNK = 64

LANES = 128
SUBLANES = 8
VMEM_LIMIT = 56 * 1024 * 1024

BF16 = jnp.bfloat16
F32 = jnp.float32

NT_DIMS = (((1,), (1,)), ((), ()))
TN_DIMS = (((0,), (0,)), ((), ()))


def _resident(shape):
    zeros = (0,) * len(shape)
    return pl.BlockSpec(shape, lambda *_: zeros, pipeline_mode=pl.Buffered(1))


def _params(*sem):
    return pltpu.CompilerParams(dimension_semantics=sem, vmem_limit_bytes=VMEM_LIMIT)


def _rms_norm(x, g):
    return x * lax.rsqrt(jnp.mean(x * x, axis=-1, keepdims=True) + EPS) * g


def _sigmoid(x):
    return 0.5 * jnp.tanh(0.5 * x) + 0.5


BF16_SUBLANES = 2 * SUBLANES


def _cast_slabs_per_job(rows, n_steps):
    return max(k for k in range(1, n_steps + 1)
               if rows % k == 0 and (rows // k) % BF16_SUBLANES == 0)


def _cast_operands(jobs, n_steps, step_of):
    ins, in_specs, out_specs, out_shapes = [], [], [], []
    for w, layer, col_ranges in jobs:
        layers, rows, cols = w.shape
        k = _cast_slabs_per_job(rows, n_steps)
        slab = rows // k
        in_map = lambda *g, k=k, layer=layer: (layer * k + jnp.minimum(step_of(*g), k - 1), 0, 0)
        out_map = lambda *g, k=k: (jnp.minimum(step_of(*g), k - 1), 0, 0)
        ins.append(w.reshape(layers * k, slab, cols))
        in_specs.append(pl.BlockSpec((1, slab, cols), in_map))
        for c0, c1 in col_ranges:
            out_specs.append(pl.BlockSpec((1, slab, c1 - c0), out_map))
            out_shapes.append(jax.ShapeDtypeStruct((k, slab, c1 - c0), BF16))
    return ins, in_specs, out_specs, out_shapes


def _cast_slabs(jobs_cols, in_refs, out_refs):
    outs = iter(out_refs)
    for ref, col_ranges in zip(in_refs, jobs_cols):
        for c0, c1 in col_ranges:
            next(outs)[0] = ref[0, :, c0:c1].astype(BF16)


def _cast_results(jobs, outs):
    outs = iter(outs)
    return [[next(outs).reshape(w.shape[1], c1 - c0) for c0, c1 in col_ranges]
            for w, _, col_ranges in jobs]


def _ab_in_kernel(x_ref, g_ref, wqkv_ref, wf_ref, wconv_ref, cw_ref,
                  qkv_ref, f_ref, b_ref, zbuf, *, tm, seq):
    i = pl.program_id(0)

    @pl.when((i * tm) % seq == 0)
    def _():
        zbuf[0:SUBLANES, :] = jnp.zeros((SUBLANES, CONV_WIDTH), F32)

    h = _rms_norm(x_ref[...], g_ref[...]).astype(BF16)
    u = jnp.dot(h, wconv_ref[...], preferred_element_type=F32)
    u_b = u[:, :CONV_WIDTH]
    z = u[:, CONV_WIDTH:2 * CONV_WIDTH] * u[:, 2 * CONV_WIDTH:]
    zbuf[SUBLANES:SUBLANES + tm, :] = z
    cw = cw_ref[...]
    y = (cw[0:1, :] * zbuf[SUBLANES - 2:SUBLANES - 2 + tm, :]
         + cw[1:2, :] * zbuf[SUBLANES - 1:SUBLANES - 1 + tm, :]
         + cw[2:3, :] * z)
    b_ref[...] = (u_b * y).astype(BF16)
    zbuf[0:SUBLANES, :] = zbuf[tm:tm + SUBLANES, :]

    f_ref[...] = jnp.dot(h, wf_ref[...], preferred_element_type=F32)
    qkv = jnp.dot(h, wqkv_ref[...], preferred_element_type=F32)
    qkv_ref[:, :FOX_WIDTH] = (qkv[:, :FOX_WIDTH] * FOX_Q_SCALE).astype(BF16)
    qkv_ref[:, FOX_WIDTH:] = qkv[:, FOX_WIDTH:].astype(BF16)


def _ab_in(x2d, g, wqkv, wf, wconv, cw, *, seq, tm=512):
    t = x2d.shape[0]
    kern = functools.partial(_ab_in_kernel, tm=tm, seq=seq)
    return pl.pallas_call(
        kern,
        grid=(t // tm,),
        in_specs=[
            pl.BlockSpec((tm, D_MODEL), lambda i: (i, 0)),
            _resident((1, D_MODEL)),
            _resident(wqkv.shape),
            _resident(wf.shape),
            _resident(wconv.shape),
            _resident(cw.shape),
        ],
        out_specs=[
            pl.BlockSpec((tm, 3 * FOX_WIDTH), lambda i: (i, 0)),
            pl.BlockSpec((tm, LANES), lambda i: (i, 0)),
            pl.BlockSpec((tm, CONV_WIDTH), lambda i: (i, 0)),
        ],
        out_shape=[
            jax.ShapeDtypeStruct((t, 3 * FOX_WIDTH), BF16),
            jax.ShapeDtypeStruct((t, LANES), F32),
            jax.ShapeDtypeStruct((t, CONV_WIDTH), BF16),
        ],
        scratch_shapes=[pltpu.VMEM((tm + SUBLANES, CONV_WIDTH), F32)],
        compiler_params=_params("arbitrary"),
        name="ab_in",
    )(x2d, g, wqkv, wf, wconv, cw)


def _fox_cumsum_kernel(f_ref, bias_ref, c_ref):
    heads, seq = c_ref.shape[1], c_ref.shape[2]
    lane = lax.broadcasted_iota(jnp.int32, (heads, LANES), 1)
    carry = jnp.zeros((heads, 1), F32)
    for j in range(seq // LANES):
        x = jnp.transpose(f_ref[0, j * LANES:(j + 1) * LANES, :])[0:heads, :] + bias_ref[...]
        blk = jnp.minimum(x, 0.0) - jnp.log(1.0 + jnp.exp(-jnp.abs(x)))
        shift = 1
        while shift < LANES:
            blk = blk + jnp.where(lane >= shift, pltpu.roll(blk, shift, 1), 0.0)
            shift *= 2
        blk = blk + carry
        c_ref[0, :, j * LANES:(j + 1) * LANES] = blk * LOG2E
        carry = blk[:, LANES - 1:LANES]


def _fox_cumsum(f_tok, bias_col):
    bsz, seq, _ = f_tok.shape
    heads = bias_col.shape[0]
    return pl.pallas_call(
        _fox_cumsum_kernel,
        grid=(bsz,),
        in_specs=[pl.BlockSpec((1, seq, LANES), lambda b: (b, 0, 0)), _resident(bias_col.shape)],
        out_specs=pl.BlockSpec((1, heads, seq), lambda b: (b, 0, 0)),
        out_shape=jax.ShapeDtypeStruct((bsz, heads, seq), F32),
        compiler_params=_params("parallel"),
        name="fox_cumsum",
    )(f_tok, bias_col)


def _fox_attn_kernel(q_ref, k_ref, v_ref, crow_ref, *rest, tq, cast_cols):
    cast_in, o_ref, cast_out = rest[:len(cast_cols)], rest[len(cast_cols)], rest[len(cast_cols) + 1:]
    _cast_slabs(cast_cols, cast_in, cast_out)
    nblk = k_ref.shape[1] // tq
    lane = lax.broadcasted_iota(jnp.int32, (1, LANES), 1)
    row = lax.broadcasted_iota(jnp.int32, (tq, tq), 0)
    col = lax.broadcasted_iota(jnp.int32, (tq, tq), 1)
    causal = row >= col
    in_head = [(lane >= hh * FOX_HEAD_DIM) & (lane < (hh + 1) * FOX_HEAD_DIM) for hh in range(2)]
    rows = lambda ref, c: ref[0, c * tq:(c + 1) * tq, :]

    for first in range(nblk // 2):
        probs = [(c, hh) for c in (first, nblk - 1 - first) for hh in range(2)]
        s = []
        for c, hh in probs:
            q = rows(q_ref, c)
            qh = jnp.where(in_head[hh], q, jnp.zeros_like(q))
            sj = []
            for j in range(c + 1):
                sh = lax.dot_general(qh, rows(k_ref, j), NT_DIMS, preferred_element_type=F32)
                sh = sh - crow_ref[0, hh, j:j + 1, :]
                sj.append(jnp.where(causal, sh, NEG_BIG) if j == c else sh)
            s.append(sj)
        shift = []
        for (c, hh), sj in zip(probs, s):
            mx = None
            for blk in sj:
                for cb in range(tq // LANES):
                    part = blk[:, cb * LANES:(cb + 1) * LANES]
                    mx = part if mx is None else jnp.maximum(mx, part)
            ct = jnp.transpose(jnp.broadcast_to(crow_ref[0, hh, c:c + 1, :], (SUBLANES, tq)))[:, 0:1]
            m = jnp.max(mx, axis=-1, keepdims=True) + ct
            shift.append(m - ct)
        outs = []
        for (c, hh), sj, sh in zip(probs, s, shift):
            acc = None
            for j in range(c + 1):
                v = rows(v_ref, j)
                v_aug = jnp.concatenate([v, jnp.ones_like(v)], axis=1)
                d = jnp.dot(jnp.exp2(sj[j] - sh).astype(BF16), v_aug, preferred_element_type=F32)
                acc = d if acc is None else acc + d
            outs.append(acc[:, :LANES] / acc[:, LANES:])
        for n, c in enumerate((first, nblk - 1 - first)):
            o_ref[0, c * tq:(c + 1) * tq, :] = jnp.where(
                lane < FOX_HEAD_DIM, outs[2 * n], outs[2 * n + 1]).astype(o_ref.dtype)


def _fox_attn(qkv, c_row, cast_jobs, *, tq):
    bsz, seq, _ = qkv.shape
    n_pairs = FOX_WIDTH // LANES
    nblk = seq // tq
    seq_block = lambda col0: pl.BlockSpec((1, seq, LANES), lambda b, p: (b, 0, col0 + p))
    c_ins, c_in_specs, c_out_specs, c_out_shapes = _cast_operands(
        cast_jobs, bsz * n_pairs, lambda b, p: b * n_pairs + p)
    outs = pl.pallas_call(
        functools.partial(_fox_attn_kernel, tq=tq, cast_cols=[job[-1] for job in cast_jobs]),
        grid=(bsz, n_pairs),
        in_specs=[
            seq_block(0),
            seq_block(n_pairs),
            seq_block(2 * n_pairs),
            pl.BlockSpec((1, 2, nblk, tq), lambda b, p: (b, p, 0, 0)),
        ] + c_in_specs,
        out_specs=[seq_block(0)] + c_out_specs,
        out_shape=[jax.ShapeDtypeStruct((bsz, seq, FOX_WIDTH), BF16)] + c_out_shapes,
        compiler_params=_params("parallel", "parallel"),
        name="fox_attn",
    )(qkv, qkv, qkv, c_row, *c_ins)
    return outs[0], _cast_results(cast_jobs, outs[1:])


def _post_kernel(*refs, n_parts, final):
    parts = refs[:n_parts]
    x_ref, wo_ref, g_ref, w1_ref, w2_ref, gf_ref, o_ref = refs[n_parts:]
    hidden = w2_ref.shape[0]
    mix = None
    off = 0
    for p_ref in parts:
        width = p_ref.shape[1]
        d = jnp.dot(p_ref[...], wo_ref[off:off + width, :], preferred_element_type=F32)
        mix = d if mix is None else mix + d
        off += width
    x1 = x_ref[...] + mix
    h = _rms_norm(x1, g_ref[...]).astype(BF16)
    gu = jnp.dot(h, w1_ref[...], preferred_element_type=F32)
    gate = gu[:, :hidden]
    act = (gate * _sigmoid(gate) * gu[:, hidden:]).astype(BF16)
    x2 = x1 + jnp.dot(act, w2_ref[...], preferred_element_type=F32)
    if final:
        x2 = _rms_norm(x2, gf_ref[...])
    o_ref[...] = x2


def _post(parts, x2d, wo, g, w1, w2, gf, *, final, tm=512):
    t = x2d.shape[0]
    kern = functools.partial(_post_kernel, n_parts=len(parts), final=final)
    in_specs = [pl.BlockSpec((tm, p.shape[1]), lambda i: (i, 0)) for p in parts]
    in_specs += [
        pl.BlockSpec((tm, D_MODEL), lambda i: (i, 0)),
        _resident(wo.shape),
        _resident((1, D_MODEL)),
        _resident(w1.shape),
        _resident(w2.shape),
        _resident((1, D_MODEL)),
    ]
    return pl.pallas_call(
        kern,
        grid=(t // tm,),
        in_specs=in_specs,
        out_specs=pl.BlockSpec((tm, D_MODEL), lambda i: (i, 0)),
        out_shape=jax.ShapeDtypeStruct((t, D_MODEL), F32),
        compiler_params=_params("parallel"),
        name="post",
    )(*parts, x2d, wo, g, w1, w2, gf)


def _hgrn_levels():
    h, out = HGRN_CHUNK // 2, []
    while h >= 1:
        out.append(h)
        h //= 2
    return out


def _hgrn_cum_table():
    t = np.arange(HGRN_CHUNK)
    m = (t[None, :] <= t[:, None]).astype(np.float32)
    return np.concatenate([m, m, m], axis=1)


def _hgrn_masks():
    c = HGRN_CHUNK
    levels = _hgrn_levels()
    row = lax.broadcasted_iota(jnp.int32, (c, HGRN_HEAD_DIM), 0)
    r2 = lax.broadcasted_iota(jnp.int32, (c, c), 0)
    c2 = lax.broadcasted_iota(jnp.int32, (c, c), 1)
    upper = [(row & h) != 0 for h in levels]
    pair = [((r2 // (2 * h)) == (c2 // (2 * h))) & ((r2 & h) != 0) & ((c2 & h) == 0)
            for h in levels]
    return row, upper, pair


def _hgrn_block(q16, f_logit, v, g16, lb, hn, cum, st, masks):
    c = HGRN_CHUNK
    levels = _hgrn_levels()
    row, upper, pair = masks

    n = q16.shape[0] // c
    chunk = lambda arr, i: arr[i * c:(i + 1) * c]
    q = q16.astype(F32)
    f = lb + (1.0 - lb) * _sigmoid(f_logit)
    k = 1.0 - f
    lg = jnp.log2(f)
    hi = lg.astype(BF16)
    r1 = lg - hi.astype(F32)
    mid = r1.astype(BF16)
    lo = (r1 - mid.astype(F32)).astype(BF16)
    b = [jnp.dot(cum, jnp.concatenate([chunk(hi, i), chunk(mid, i), chunk(lo, i)], axis=0),
                 preferred_element_type=F32) for i in range(n)]

    a = [None] * n
    for li, h in enumerate(levels):
        xs = []
        for i in range(n):
            qi, ki, bi = chunk(q, i), chunk(k, i), b[i]
            if h == 1:
                x = jnp.where(upper[li], qi * chunk(f, i), ki)
            else:
                if h >= SUBLANES:
                    ref = jnp.concatenate(
                        [jnp.broadcast_to(bi[base + h - 1:base + h, :], (2 * h, HGRN_HEAD_DIM))
                         for base in range(0, c, 2 * h)], axis=0)
                else:
                    b3 = bi.reshape(c // SUBLANES, SUBLANES, HGRN_HEAD_DIM)
                    refs = [jnp.broadcast_to(b3[:, base + h - 1:base + h, :], b3.shape).reshape(bi.shape)
                            for base in range(0, SUBLANES, 2 * h)]
                    ref = refs[0]
                    for m in range(1, len(refs)):
                        ref = jnp.where((row % SUBLANES) >= m * 2 * h, refs[m], ref)
                x = jnp.where(upper[li], qi, ki) * jnp.exp2(-jnp.abs(bi - ref))
            xs.append(x.astype(BF16))
        ps = [lax.dot_general(x, x, NT_DIMS, preferred_element_type=F32) for x in xs]
        a = [jnp.where(pair[li], ps[i], 0.0 if a[i] is None else a[i]) for i in range(n)]
    o_intra = [jnp.dot(a[i].astype(BF16), chunk(v, i), preferred_element_type=F32) for i in range(n)]

    b_last = [b[i][c - 1:c, :] for i in range(n)]
    kv = [lax.dot_general(chunk(v, i), (chunk(k, i) * jnp.exp2(b_last[i] - b[i])).astype(BF16),
                          TN_DIMS, preferred_element_type=F32) for i in range(n)]
    o_inter = []
    for i in range(n):
        o_inter.append(lax.dot_general((chunk(q, i) * jnp.exp2(b[i])).astype(BF16), st.astype(BF16),
                                       NT_DIMS, preferred_element_type=F32))
        st = st * jnp.exp2(b_last[i]) + kv[i]

    o = jnp.concatenate([o_inter[i] + o_intra[i] for i in range(n)], axis=0)
    o = o + jnp.sum(q * k, axis=-1, keepdims=True) * v.astype(F32)
    o = o * lax.rsqrt(jnp.mean(o * o, axis=-1, keepdims=True) + EPS) * hn
    g = g16.astype(F32)
    return (o * (g * _sigmoid(g))).astype(BF16), st


def _c_in_kernel(x_ref, g_ref, w_ref, o_ref, f_ref):
    h = _rms_norm(x_ref[...], g_ref[...]).astype(BF16)
    pr = jnp.dot(h, w_ref[...], preferred_element_type=F32)
    o_ref[:, :D_MODEL] = pr[:, :D_MODEL].astype(BF16)
    o_ref[:, D_MODEL:] = pr[:, 2 * D_MODEL:].astype(BF16)
    f_ref[...] = pr[:, D_MODEL:2 * D_MODEL]


def _c_in(x2d, g, w, *, tm=512):
    t = x2d.shape[0]
    return pl.pallas_call(
        _c_in_kernel,
        grid=(t // tm,),
        in_specs=[
            pl.BlockSpec((tm, D_MODEL), lambda i: (i, 0)),
            _resident((1, D_MODEL)),
            _resident(w.shape),
        ],
        out_specs=[
            pl.BlockSpec((tm, 3 * D_MODEL), lambda i: (i, 0)),
            pl.BlockSpec((tm, D_MODEL), lambda i: (i, 0)),
        ],
        out_shape=[
            jax.ShapeDtypeStruct((t, 3 * D_MODEL), BF16),
            jax.ShapeDtypeStruct((t, D_MODEL), F32),
        ],
        compiler_params=_params("parallel"),
        name="c_in",
    )(x2d, g, w)


def _hgrn_kernel(q_ref, f_ref, i_ref, g_ref, lbraw_ref, hn_ref, cum_ref, *rest,
                 layer_idx, cast_cols):
    n_cast = len(cast_cols)
    cast_in, o_ref, cast_out, state = rest[:n_cast], rest[n_cast], rest[n_cast + 1:-1], rest[-1]
    _cast_slabs(cast_cols, cast_in, cast_out)

    @pl.when(pl.program_id(2) == 0)
    def _():
        state[...] = jnp.zeros_like(state)

    raw = lbraw_ref[...]
    e = jnp.exp(raw - jnp.max(raw, axis=0, keepdims=True))
    sm = e / jnp.sum(e, axis=0, keepdims=True)
    lb = jnp.sum(sm[0:layer_idx + 1, :], axis=0, keepdims=True) - sm[0:1, :]
    o, st = _hgrn_block(q_ref[0], f_ref[0], i_ref[0], g_ref[0], lb, hn_ref[...], cum_ref[...],
                        state[...], _hgrn_masks())
    state[...] = st
    o_ref[0] = o


def _hgrn(proj, f_logit, lb_raw, head_norm, cast_jobs, *, layer_idx, blk=2048):
    bsz, seq, _ = f_logit.shape
    hd = HGRN_HEAD_DIM
    nh = HGRN_HEADS
    n_l = seq // blk
    cum = jnp.asarray(_hgrn_cum_table(), BF16)
    c_ins, c_in_specs, c_out_specs, c_out_shapes = _cast_operands(
        cast_jobs, bsz * nh * n_l, lambda b, h, l: (b * nh + h) * n_l + l)
    outs = pl.pallas_call(
        functools.partial(_hgrn_kernel, layer_idx=layer_idx,
                          cast_cols=[job[-1] for job in cast_jobs]),
        grid=(bsz, nh, n_l),
        in_specs=[
            pl.BlockSpec((1, blk, hd), lambda b, h, l: (b, l, h)),
            pl.BlockSpec((1, blk, hd), lambda b, h, l: (b, l, h)),
            pl.BlockSpec((1, blk, hd), lambda b, h, l: (b, l, nh + h)),
            pl.BlockSpec((1, blk, hd), lambda b, h, l: (b, l, 2 * nh + h)),
            pl.BlockSpec((lb_raw.shape[0], hd), lambda b, h, l: (0, h)),
            pl.BlockSpec((1, hd), lambda b, h, l: (0, h)),
            _resident(cum.shape),
        ] + c_in_specs,
        out_specs=[pl.BlockSpec((1, blk, hd), lambda b, h, l: (b, l, h))] + c_out_specs,
        out_shape=[jax.ShapeDtypeStruct((bsz, seq, D_MODEL), BF16)] + c_out_shapes,
        scratch_shapes=[pltpu.VMEM((hd, hd), F32)],
        compiler_params=_params("parallel", "parallel", "arbitrary"),
        name="hgrn",
    )(proj, f_logit, proj, proj, lb_raw, head_norm, cum, *c_ins)
    return outs[0], _cast_results(cast_jobs, outs[1:])


def kernel(x, norm_mix, norm_ffn, final_norm, ab_w_in, fox_f_bias, conv_w, ab_w_out,
           c_w_in, c_lower_bounds, c_head_norm, c_w_out, ffn_w_in, ffn_w_out):
    bsz, seq, d = x.shape
    depth = norm_mix.shape[0]
    t = bsz * seq
    x2d = x.reshape(t, d)
    gf = final_norm.reshape(1, d)
    q_end = 3 * FOX_WIDTH
    f_end = q_end + FOX_HEADS

    ab_cols = [(0, q_end), (q_end, q_end + LANES), (f_end, f_end + 3 * CONV_WIDTH)]
    whole = lambda w, layer: (w, layer, [(0, w.shape[2])])

    def in_proj_job(layer):
        j = layer // 2
        return (ab_w_in, j, ab_cols) if layer % 2 == 0 else whole(c_w_in, j)

    w0, j0, cols0 = in_proj_job(0)
    in_w = [w0[j0, :, c0:c1].astype(BF16) for c0, c1 in cols0]

    for layer in range(depth):
        j = layer // 2
        g_mix = norm_mix[layer].reshape(1, d)
        jobs = [whole(ab_w_out if layer % 2 == 0 else c_w_out, j),
                whole(ffn_w_in, layer), whole(ffn_w_out, layer)]
        if layer + 1 < depth:
            jobs.append(in_proj_job(layer + 1))
        if layer % 2 == 0:
            wqkv, wf, wconv = in_w
            qkv, f_pad, b_out = _ab_in(x2d, g_mix, wqkv, wf, wconv, conv_w[j], seq=seq)
            c = _fox_cumsum(f_pad.reshape(bsz, seq, LANES), fox_f_bias[j].reshape(FOX_HEADS, 1))
            tq = FOX_Q_TILE
            c_row = c.reshape(bsz, FOX_HEADS, seq // tq, tq)
            a_out, cast = _fox_attn(qkv.reshape(bsz, seq, q_end), c_row, jobs, tq=tq)
            parts = [a_out.reshape(t, FOX_WIDTH), b_out]
        else:
            proj, f_logit = _c_in(x2d, g_mix, in_w[0])
            o, cast = _hgrn(proj.reshape(bsz, seq, 3 * d), f_logit.reshape(bsz, seq, d),
                            c_lower_bounds, c_head_norm[j].reshape(1, d), jobs, layer_idx=j)
            parts = [o.reshape(t, d)]
        (wo,), (w1,), (w2,) = cast[:3]
        in_w = cast[3] if layer + 1 < depth else None
        x2d = _post(parts, x2d, wo, norm_ffn[layer].reshape(1, d), w1, w2, gf,
                    final=(layer == depth - 1))
    return x2d.reshape(bsz, seq, d)
```

```python
import functools

import jax
import jax.numpy as jnp
import numpy as np
from jax import lax
from jax.experimental import pallas as pl
from jax.experimental.pallas import tpu as pltpu

D_MODEL = 1024
EPS = 1e-6
NEG_BIG = -1e30

FOX_HEADS = 8
FOX_HEAD_DIM = 64
FOX_WIDTH = FOX_HEADS * FOX_HEAD_DIM
CONV_WIDTH = D_MODEL - FOX_WIDTH
CONV_TAPS = 3
LOG2E = 1.4426950408889634
FOX_Q_SCALE = LOG2E * FOX_HEAD_DIM ** -0.5
FOX_Q_TILE = 256
HGRN_HEADS = 8
HGRN_HEAD_DIM = D_MODEL // HGRN_HEADS
HGRN_CHUNK = 64

LANES = 128
SUBLANES = 8
VMEM_LIMIT = 56 * 1024 * 1024

BF16 = jnp.bfloat16
F32 = jnp.float32

NT_DIMS = (((1,), (1,)), ((), ()))
TN_DIMS = (((0,), (0,)), ((), ()))


def _resident(shape):
    zeros = (0,) * len(shape)
    return pl.BlockSpec(shape, lambda *_: zeros, pipeline_mode=pl.Buffered(1))


def _params(*sem):
    return pltpu.CompilerParams(dimension_semantics=sem, vmem_limit_bytes=VMEM_LIMIT)


def _rms_norm(x, g):
    return x * lax.rsqrt(jnp.mean(x * x, axis=-1, keepdims=True) + EPS) * g


def _silu(x):
    half = 0.5 * x
    return half * (jnp.tanh(half) + 1.0)


BF16_SUBLANES = 2 * SUBLANES


def _cast_slabs_per_job(rows, n_steps):
    return max(k for k in range(1, n_steps + 1)
               if rows % k == 0 and (rows // k) % BF16_SUBLANES == 0)


def _cast_operands(jobs, n_steps, step_of):
    ins, in_specs, out_specs, out_shapes = [], [], [], []
    for w, layer, col_ranges in jobs:
        layers, rows, cols = w.shape
        k = _cast_slabs_per_job(rows, n_steps)
        slab = rows // k
        in_map = lambda *g, k=k, layer=layer: (layer * k + jnp.minimum(step_of(*g), k - 1), 0, 0)
        out_map = lambda *g, k=k: (jnp.minimum(step_of(*g), k - 1), 0, 0)
        ins.append(w.reshape(layers * k, slab, cols))
        in_specs.append(pl.BlockSpec((1, slab, cols), in_map))
        for c0, c1 in col_ranges:
            out_specs.append(pl.BlockSpec((1, slab, c1 - c0), out_map))
            out_shapes.append(jax.ShapeDtypeStruct((k, slab, c1 - c0), BF16))
    return ins, in_specs, out_specs, out_shapes


def _cast_slabs(jobs_cols, in_refs, out_refs):
    outs = iter(out_refs)
    for ref, col_ranges in zip(in_refs, jobs_cols):
        for c0, c1 in col_ranges:
            next(outs)[0] = ref[0, :, c0:c1].astype(BF16)


def _cast_results(jobs, outs):
    outs = iter(outs)
    return [[next(outs).reshape(w.shape[1], c1 - c0) for c0, c1 in col_ranges]
            for w, _, col_ranges in jobs]


def _ab_in_kernel(x_ref, g_ref, wqkv_ref, wf_ref, wconv_ref, cw_ref,
                  qkv_ref, f_ref, b_ref, zbuf, *, tm, seq):
    i = pl.program_id(0)

    @pl.when((i * tm) % seq == 0)
    def _():
        zbuf[0:SUBLANES, :] = jnp.zeros((SUBLANES, CONV_WIDTH), F32)

    h = _rms_norm(x_ref[...], g_ref[...]).astype(BF16)
    u = jnp.dot(h, wconv_ref[...], preferred_element_type=F32)
    u_b = u[:, :CONV_WIDTH]
    z = u[:, CONV_WIDTH:2 * CONV_WIDTH] * u[:, 2 * CONV_WIDTH:]
    zbuf[SUBLANES:SUBLANES + tm, :] = z
    cw = cw_ref[...]
    y = (cw[0:1, :] * zbuf[SUBLANES - 2:SUBLANES - 2 + tm, :]
         + cw[1:2, :] * zbuf[SUBLANES - 1:SUBLANES - 1 + tm, :]
         + cw[2:3, :] * z)
    b_ref[...] = (u_b * y).astype(BF16)
    zbuf[0:SUBLANES, :] = zbuf[tm:tm + SUBLANES, :]

    f_ref[...] = jnp.dot(h, wf_ref[...], preferred_element_type=F32)
    qkv = jnp.dot(h, wqkv_ref[...], preferred_element_type=F32)
    qkv_ref[:, :FOX_WIDTH] = (qkv[:, :FOX_WIDTH] * FOX_Q_SCALE).astype(BF16)
    qkv_ref[:, FOX_WIDTH:] = qkv[:, FOX_WIDTH:].astype(BF16)


def _ab_in(x2d, g, wqkv, wf, wconv, cw, *, seq, tm=512):
    t = x2d.shape[0]
    kern = functools.partial(_ab_in_kernel, tm=tm, seq=seq)
    return pl.pallas_call(
        kern,
        grid=(t // tm,),
        in_specs=[
            pl.BlockSpec((tm, D_MODEL), lambda i: (i, 0)),
            _resident((1, D_MODEL)),
            _resident(wqkv.shape),
            _resident(wf.shape),
            _resident(wconv.shape),
            _resident(cw.shape),
        ],
        out_specs=[
            pl.BlockSpec((tm, 3 * FOX_WIDTH), lambda i: (i, 0)),
            pl.BlockSpec((tm, LANES), lambda i: (i, 0)),
            pl.BlockSpec((tm, CONV_WIDTH), lambda i: (i, 0)),
        ],
        out_shape=[
            jax.ShapeDtypeStruct((t, 3 * FOX_WIDTH), BF16),
            jax.ShapeDtypeStruct((t, LANES), F32),
            jax.ShapeDtypeStruct((t, CONV_WIDTH), BF16),
        ],
        scratch_shapes=[pltpu.VMEM((tm + SUBLANES, CONV_WIDTH), F32)],
        compiler_params=_params("arbitrary"),
        name="ab_in",
    )(x2d, g, wqkv, wf, wconv, cw)


def _fox_cumsum_kernel(f_ref, bias_ref, c_ref):
    heads, seq = c_ref.shape[1], c_ref.shape[2]
    lane = lax.broadcasted_iota(jnp.int32, (heads, LANES), 1)
    carry = jnp.zeros((heads, 1), F32)
    for j in range(seq // LANES):
        x = jnp.transpose(f_ref[0, j * LANES:(j + 1) * LANES, :])[0:heads, :] + bias_ref[...]
        blk = jnp.minimum(x, 0.0) - jnp.log(1.0 + jnp.exp(-jnp.abs(x)))
        shift = 1
        while shift < LANES:
            blk = blk + jnp.where(lane >= shift, pltpu.roll(blk, shift, 1), 0.0)
            shift *= 2
        blk = blk + carry
        c_ref[0, :, j * LANES:(j + 1) * LANES] = blk * LOG2E
        carry = blk[:, LANES - 1:LANES]


def _fox_cumsum(f_tok, bias_col):
    bsz, seq, _ = f_tok.shape
    heads = bias_col.shape[0]
    return pl.pallas_call(
        _fox_cumsum_kernel,
        grid=(bsz,),
        in_specs=[pl.BlockSpec((1, seq, LANES), lambda b: (b, 0, 0)), _resident(bias_col.shape)],
        out_specs=pl.BlockSpec((1, heads, seq), lambda b: (b, 0, 0)),
        out_shape=jax.ShapeDtypeStruct((bsz, heads, seq), F32),
        compiler_params=_params("parallel"),
        name="fox_cumsum",
    )(f_tok, bias_col)


def _fox_attn_kernel(q_ref, k_ref, v_ref, crow_ref, *rest, tq, cast_cols):
    cast_in, o_ref, cast_out = rest[:len(cast_cols)], rest[len(cast_cols)], rest[len(cast_cols) + 1:]
    _cast_slabs(cast_cols, cast_in, cast_out)
    nblk = k_ref.shape[1] // tq
    lane = lax.broadcasted_iota(jnp.int32, (1, LANES), 1)
    row = lax.broadcasted_iota(jnp.int32, (tq, tq), 0)
    col = lax.broadcasted_iota(jnp.int32, (tq, tq), 1)
    causal = row >= col
    in_head = [(lane >= hh * FOX_HEAD_DIM) & (lane < (hh + 1) * FOX_HEAD_DIM) for hh in range(2)]
    rows = lambda ref, c: ref[0, c * tq:(c + 1) * tq, :]

    for first in range(nblk // 2):
        probs = [(c, hh) for c in (first, nblk - 1 - first) for hh in range(2)]
        s = []
        for c, hh in probs:
            q = rows(q_ref, c)
            qh = jnp.where(in_head[hh], q, jnp.zeros_like(q))
            sj = []
            for j in range(c + 1):
                sh = lax.dot_general(qh, rows(k_ref, j), NT_DIMS, preferred_element_type=F32)
                sh = sh - crow_ref[0, hh, j:j + 1, :]
                sj.append(jnp.where(causal, sh, NEG_BIG) if j == c else sh)
            s.append(sj)
        shift = []
        for (c, hh), sj in zip(probs, s):
            mx = None
            for blk in sj:
                for cb in range(tq // LANES):
                    part = blk[:, cb * LANES:(cb + 1) * LANES]
                    mx = part if mx is None else jnp.maximum(mx, part)
            ct = jnp.transpose(jnp.broadcast_to(crow_ref[0, hh, c:c + 1, :], (SUBLANES, tq)))[:, 0:1]
            m = jnp.max(mx, axis=-1, keepdims=True) + ct
            shift.append(m - ct)
        outs = []
        for (c, hh), sj, sh in zip(probs, s, shift):
            acc = None
            for j in range(c + 1):
                v = rows(v_ref, j)
                v_aug = jnp.concatenate([v, jnp.ones_like(v)], axis=1)
                d = jnp.dot(jnp.exp2(sj[j] - sh).astype(BF16), v_aug, preferred_element_type=F32)
                acc = d if acc is None else acc + d
            outs.append(acc[:, :LANES] / acc[:, LANES:])
        for n, c in enumerate((first, nblk - 1 - first)):
            o_ref[0, c * tq:(c + 1) * tq, :] = jnp.where(
                lane < FOX_HEAD_DIM, outs[2 * n], outs[2 * n + 1]).astype(o_ref.dtype)


def _fox_attn(qkv, c_row, cast_jobs, *, tq):
    bsz, seq, _ = qkv.shape
    n_pairs = FOX_WIDTH // LANES
    nblk = seq // tq
    seq_block = lambda col0: pl.BlockSpec((1, seq, LANES), lambda b, p: (b, 0, col0 + p))
    c_ins, c_in_specs, c_out_specs, c_out_shapes = _cast_operands(
        cast_jobs, bsz * n_pairs, lambda b, p: b * n_pairs + p)
    outs = pl.pallas_call(
        functools.partial(_fox_attn_kernel, tq=tq, cast_cols=[job[-1] for job in cast_jobs]),
        grid=(bsz, n_pairs),
        in_specs=[
            seq_block(0),
            seq_block(n_pairs),
            seq_block(2 * n_pairs),
            pl.BlockSpec((1, 2, nblk, tq), lambda b, p: (b, p, 0, 0)),
        ] + c_in_specs,
        out_specs=[seq_block(0)] + c_out_specs,
        out_shape=[jax.ShapeDtypeStruct((bsz, seq, FOX_WIDTH), BF16)] + c_out_shapes,
        compiler_params=_params("parallel", "parallel"),
        name="fox_attn",
    )(qkv, qkv, qkv, c_row, *c_ins)
    return outs[0], _cast_results(cast_jobs, outs[1:])


def _post_kernel(*refs, n_parts, final):
    parts = refs[:n_parts]
    x_ref, wo_ref, g_ref, w1_ref, w2_ref, gf_ref, o_ref = refs[n_parts:]
    hidden = w2_ref.shape[0]
    mix = None
    off = 0
    for p_ref in parts:
        width = p_ref.shape[1]
        d = jnp.dot(p_ref[...], wo_ref[off:off + width, :], preferred_element_type=F32)
        mix = d if mix is None else mix + d
        off += width
    x1 = x_ref[...] + mix
    h = _rms_norm(x1, g_ref[...]).astype(BF16)
    gu = jnp.dot(h, w1_ref[...], preferred_element_type=F32)
    gate = gu[:, :hidden]
    act = (_silu(gate) * gu[:, hidden:]).astype(BF16)
    x2 = x1 + jnp.dot(act, w2_ref[...], preferred_element_type=F32)
    if final:
        x2 = _rms_norm(x2, gf_ref[...])
    o_ref[...] = x2


def _post(parts, x2d, wo, g, w1, w2, gf, *, final, tm=512):
    t = x2d.shape[0]
    kern = functools.partial(_post_kernel, n_parts=len(parts), final=final)
    in_specs = [pl.BlockSpec((tm, p.shape[1]), lambda i: (i, 0)) for p in parts]
    in_specs += [
        pl.BlockSpec((tm, D_MODEL), lambda i: (i, 0)),
        _resident(wo.shape),
        _resident((1, D_MODEL)),
        _resident(w1.shape),
        _resident(w2.shape),
        _resident((1, D_MODEL)),
    ]
    return pl.pallas_call(
        kern,
        grid=(t // tm,),
        in_specs=in_specs,
        out_specs=pl.BlockSpec((tm, D_MODEL), lambda i: (i, 0)),
        out_shape=jax.ShapeDtypeStruct((t, D_MODEL), F32),
        compiler_params=_params("parallel"),
        name="post",
    )(*parts, x2d, wo, g, w1, w2, gf)


def _hgrn_levels():
    h, out = HGRN_CHUNK // 2, []
    while h >= 1:
        out.append(h)
        h //= 2
    return out


def _hgrn_cum_table():
    t = np.arange(HGRN_CHUNK)
    m = (t[None, :] <= t[:, None]).astype(np.float32)
    return np.concatenate([m, m, m], axis=1)


def _hgrn_masks():
    c = HGRN_CHUNK
    levels = _hgrn_levels()
    row = lax.broadcasted_iota(jnp.int32, (c, HGRN_HEAD_DIM), 0)
    r2 = lax.broadcasted_iota(jnp.int32, (c, c), 0)
    c2 = lax.broadcasted_iota(jnp.int32, (c, c), 1)
    upper = [(row & h) != 0 for h in levels]
    pair = [((r2 // (2 * h)) == (c2 // (2 * h))) & ((r2 & h) != 0) & ((c2 & h) == 0)
            for h in levels]
    sign = [jnp.where(u, 1.0, -1.0) for u in upper]
    return row, upper, pair, sign, r2 == c2


def _hgrn_block(q16, f_logit, v, g16, lb, hn, cum, st, masks):
    c = HGRN_CHUNK
    levels = _hgrn_levels()
    row, upper, pair, sign, diag = masks

    n = q16.shape[0] // c
    chunk = lambda arr, i: arr[i * c:(i + 1) * c]
    q = q16.astype(F32)
    half_gap = 0.5 * (1.0 - lb)
    swing = half_gap * jnp.tanh(0.5 * f_logit)
    f = (1.0 - half_gap) + swing
    k = half_gap - swing
    lg = jnp.log2(f)
    hi = lg.astype(BF16)
    r1 = lg - hi.astype(F32)
    mid = r1.astype(BF16)
    lo = (r1 - mid.astype(F32)).astype(BF16)
    b = [jnp.dot(cum, jnp.concatenate([chunk(hi, i), chunk(mid, i), chunk(lo, i)], axis=0),
                 preferred_element_type=F32) for i in range(n)]

    qk = jnp.sum(q * k, axis=-1, keepdims=True)

    a = [None] * n
    o_inter = []
    for li, h in enumerate(levels):
        xs = []
        for i in range(n):
            qi, ki, bi = chunk(q, i), chunk(k, i), b[i]
            if h == 1:
                x = jnp.where(upper[li], qi * chunk(f, i), ki)
            else:
                if h >= SUBLANES:
                    ref = jnp.concatenate(
                        [jnp.broadcast_to(bi[base + h - 1:base + h, :], (2 * h, HGRN_HEAD_DIM))
                         for base in range(0, c, 2 * h)], axis=0)
                else:
                    b3 = bi.reshape(c // SUBLANES, SUBLANES, HGRN_HEAD_DIM)
                    refs = [jnp.broadcast_to(b3[:, base + h - 1:base + h, :], b3.shape).reshape(bi.shape)
                            for base in range(0, SUBLANES, 2 * h)]
                    ref = refs[0]
                    for m in range(1, len(refs)):
                        ref = jnp.where((row % SUBLANES) >= m * 2 * h, refs[m], ref)
                x = jnp.where(upper[li], qi, ki) * jnp.exp2((bi - ref) * sign[li])
            xs.append(x.astype(BF16))
        ps = [lax.dot_general(x, x, NT_DIMS, preferred_element_type=F32) for x in xs]
        a = [jnp.where(pair[li], ps[i], 0.0 if a[i] is None else a[i]) for i in range(n)]

        for i in range(li * n // len(levels), (li + 1) * n // len(levels)):
            b_last = b[i][c - 1:c, :]
            kv = lax.dot_general(chunk(v, i), (chunk(k, i) * jnp.exp2(b_last - b[i])).astype(BF16),
                                 TN_DIMS, preferred_element_type=F32)
            o_inter.append(lax.dot_general((chunk(q, i) * jnp.exp2(b[i])).astype(BF16),
                                           st.astype(BF16), NT_DIMS, preferred_element_type=F32))
            st = st * jnp.exp2(b_last) + kv
    a = [jnp.where(diag, chunk(qk, i), a[i]) for i in range(n)]
    o_intra = [jnp.dot(a[i].astype(BF16), chunk(v, i), preferred_element_type=F32) for i in range(n)]

    o = jnp.concatenate([o_inter[i] + o_intra[i] for i in range(n)], axis=0)
    o = o * lax.rsqrt(jnp.mean(o * o, axis=-1, keepdims=True) + EPS) * hn
    g = g16.astype(F32)
    return (o * _silu(g)).astype(BF16), st


def _c_in_kernel(x_ref, g_ref, w_ref, o_ref, f_ref):
    h = _rms_norm(x_ref[...], g_ref[...]).astype(BF16)
    pr = jnp.dot(h, w_ref[...], preferred_element_type=F32)
    o_ref[:, :D_MODEL] = pr[:, :D_MODEL].astype(BF16)
    o_ref[:, D_MODEL:] = pr[:, 2 * D_MODEL:].astype(BF16)
    f_ref[...] = pr[:, D_MODEL:2 * D_MODEL]


def _c_in(x2d, g, w, *, tm=512):
    t = x2d.shape[0]
    return pl.pallas_call(
        _c_in_kernel,
        grid=(t // tm,),
        in_specs=[
            pl.BlockSpec((tm, D_MODEL), lambda i: (i, 0)),
            _resident((1, D_MODEL)),
            _resident(w.shape),
        ],
        out_specs=[
            pl.BlockSpec((tm, 3 * D_MODEL), lambda i: (i, 0)),
            pl.BlockSpec((tm, D_MODEL), lambda i: (i, 0)),
        ],
        out_shape=[
            jax.ShapeDtypeStruct((t, 3 * D_MODEL), BF16),
            jax.ShapeDtypeStruct((t, D_MODEL), F32),
        ],
        compiler_params=_params("parallel"),
        name="c_in",
    )(x2d, g, w)


HGRN_HEADS_PER_STEP = 2


def _hgrn_kernel(q_ref, f_ref, i_ref, g_ref, lbraw_ref, hn_ref, cum_ref, *rest,
                 layer_idx, cast_cols):
    n_cast = len(cast_cols)
    cast_in, o_ref, cast_out, state = rest[:n_cast], rest[n_cast], rest[n_cast + 1:-1], rest[-1]
    _cast_slabs(cast_cols, cast_in, cast_out)

    @pl.when(pl.program_id(2) == 0)
    def _():
        state[...] = jnp.zeros_like(state)

    raw = lbraw_ref[...]
    e = jnp.exp(raw - jnp.max(raw, axis=0, keepdims=True))
    sm = e / jnp.sum(e, axis=0, keepdims=True)
    lb = jnp.sum(sm[0:layer_idx + 1, :], axis=0, keepdims=True) - sm[0:1, :]
    hn = hn_ref[...]
    masks = _hgrn_masks()
    for n in range(HGRN_HEADS_PER_STEP):
        hs = slice(n * HGRN_HEAD_DIM, (n + 1) * HGRN_HEAD_DIM)
        o, st = _hgrn_block(q_ref[0, :, hs], f_ref[0, :, hs], i_ref[0, :, hs], g_ref[0, :, hs],
                            lb[:, hs], hn[:, hs], cum_ref[...], state[n], masks)
        state[n] = st
        o_ref[0, :, hs] = o


def _hgrn(proj, f_logit, lb_raw, head_norm, cast_jobs, *, layer_idx, blk=2048):
    bsz, seq, _ = f_logit.shape
    hd = HGRN_HEAD_DIM
    per = HGRN_HEADS_PER_STEP
    width = per * hd
    ng = HGRN_HEADS // per
    n_l = seq // blk
    cum = jnp.asarray(_hgrn_cum_table(), BF16)
    c_ins, c_in_specs, c_out_specs, c_out_shapes = _cast_operands(
        cast_jobs, bsz * ng * n_l, lambda b, h, l: (b * ng + h) * n_l + l)
    outs = pl.pallas_call(
        functools.partial(_hgrn_kernel, layer_idx=layer_idx,
                          cast_cols=[job[-1] for job in cast_jobs]),
        grid=(bsz, ng, n_l),
        in_specs=[
            pl.BlockSpec((1, blk, width), lambda b, h, l: (b, l, h)),
            pl.BlockSpec((1, blk, width), lambda b, h, l: (b, l, h)),
            pl.BlockSpec((1, blk, width), lambda b, h, l: (b, l, ng + h)),
            pl.BlockSpec((1, blk, width), lambda b, h, l: (b, l, 2 * ng + h)),
            pl.BlockSpec((lb_raw.shape[0], width), lambda b, h, l: (0, h)),
            pl.BlockSpec((1, width), lambda b, h, l: (0, h)),
            _resident(cum.shape),
        ] + c_in_specs,
        out_specs=[pl.BlockSpec((1, blk, width), lambda b, h, l: (b, l, h))] + c_out_specs,
        out_shape=[jax.ShapeDtypeStruct((bsz, seq, D_MODEL), BF16)] + c_out_shapes,
        scratch_shapes=[pltpu.VMEM((per, hd, hd), F32)],
        compiler_params=_params("parallel", "parallel", "arbitrary"),
        name="hgrn",
    )(proj, f_logit, proj, proj, lb_raw, head_norm, cum, *c_ins)
    return outs[0], _cast_results(cast_jobs, outs[1:])


def kernel(x, norm_mix, norm_ffn, final_norm, ab_w_in, fox_f_bias, conv_w, ab_w_out,
           c_w_in, c_lower_bounds, c_head_norm, c_w_out, ffn_w_in, ffn_w_out):
    bsz, seq, d = x.shape
    depth = norm_mix.shape[0]
    t = bsz * seq
    x2d = x.reshape(t, d)
    gf = final_norm.reshape(1, d)
    q_end = 3 * FOX_WIDTH
    f_end = q_end + FOX_HEADS

    ab_cols = [(0, q_end), (q_end, q_end + LANES), (f_end, f_end + 3 * CONV_WIDTH)]
    whole = lambda w, layer: (w, layer, [(0, w.shape[2])])

    def in_proj_job(layer):
        j = layer // 2
        return (ab_w_in, j, ab_cols) if layer % 2 == 0 else whole(c_w_in, j)

    w0, j0, cols0 = in_proj_job(0)
    in_w = [w0[j0, :, c0:c1].astype(BF16) for c0, c1 in cols0]

    for layer in range(depth):
        j = layer // 2
        g_mix = norm_mix[layer].reshape(1, d)
        jobs = [whole(ab_w_out if layer % 2 == 0 else c_w_out, j),
                whole(ffn_w_in, layer), whole(ffn_w_out, layer)]
        if layer + 1 < depth:
            jobs.append(in_proj_job(layer + 1))
        if layer % 2 == 0:
            wqkv, wf, wconv = in_w
            qkv, f_pad, b_out = _ab_in(x2d, g_mix, wqkv, wf, wconv, conv_w[j], seq=seq)
            c = _fox_cumsum(f_pad.reshape(bsz, seq, LANES), fox_f_bias[j].reshape(FOX_HEADS, 1))
            tq = FOX_Q_TILE
            c_row = c.reshape(bsz, FOX_HEADS, seq // tq, tq)
            a_out, cast = _fox_attn(qkv.reshape(bsz, seq, q_end), c_row, jobs, tq=tq)
            parts = [a_out.reshape(t, FOX_WIDTH), b_out]
        else:
            proj, f_logit = _c_in(x2d, g_mix, in_w[0])
            o, cast = _hgrn(proj.reshape(bsz, seq, 3 * d), f_logit.reshape(bsz, seq, d),
                            c_lower_bounds, c_head_norm[j].reshape(1, d), jobs, layer_idx=j)
            parts = [o.reshape(t, d)]
        (wo,), (w1,), (w2,) = cast[:3]
        in_w = cast[3] if layer + 1 < depth else None
        x2d = _post(parts, x2d, wo, norm_ffn[layer].reshape(1, d), w1, w2, gf,
                    final=(layer == depth - 1))
    return x2d.reshape(bsz, seq, d)
```

```python
import functools

import jax
import jax.numpy as jnp
import numpy as np
from jax import lax
from jax.experimental import pallas as pl
from jax.experimental.pallas import tpu as pltpu

D_MODEL = 1024
EPS = 1e-6
NEG_BIG = -1e30

FOX_HEADS = 8
FOX_HEAD_DIM = 64
FOX_WIDTH = FOX_HEADS * FOX_HEAD_DIM
CONV_WIDTH = D_MODEL - FOX_WIDTH
CONV_TAPS = 3
LOG2E = 1.4426950408889634
FOX_Q_SCALE = LOG2E * FOX_HEAD_DIM ** -0.5
FOX_Q_TILE = 256
HGRN_HEADS = 8
HGRN_HEAD_DIM = D_MODEL // HGRN_HEADS
HGRN_CHUNK = 64

LANES = 128
SUBLANES = 8
VMEM_LIMIT = 56 * 1024 * 1024

BF16 = jnp.bfloat16
F32 = jnp.float32

NT_DIMS = (((1,), (1,)), ((), ()))
TN_DIMS = (((0,), (0,)), ((), ()))


def _resident(shape):
    zeros = (0,) * len(shape)
    return pl.BlockSpec(shape, lambda *_: zeros, pipeline_mode=pl.Buffered(1))


def _params(*sem):
    return pltpu.CompilerParams(dimension_semantics=sem, vmem_limit_bytes=VMEM_LIMIT)


def _rms_norm(x, g):
    return x * lax.rsqrt(jnp.mean(x * x, axis=-1, keepdims=True) + EPS) * g


def _silu(x):
    half = 0.5 * x
    return half * (jnp.tanh(half) + 1.0)


BF16_SUBLANES = 2 * SUBLANES


def _cast_slabs_per_job(rows, n_steps):
    return max(k for k in range(1, n_steps + 1)
               if rows % k == 0 and (rows // k) % BF16_SUBLANES == 0)


def _cast_operands(jobs, n_steps, step_of):
    ins, in_specs, out_specs, out_shapes = [], [], [], []
    for w, layer, col_ranges in jobs:
        layers, rows, cols = w.shape
        k = _cast_slabs_per_job(rows, n_steps)
        slab = rows // k
        in_map = lambda *g, k=k, layer=layer: (layer * k + jnp.minimum(step_of(*g), k - 1), 0, 0)
        out_map = lambda *g, k=k: (jnp.minimum(step_of(*g), k - 1), 0, 0)
        ins.append(w.reshape(layers * k, slab, cols))
        in_specs.append(pl.BlockSpec((1, slab, cols), in_map))
        for c0, c1 in col_ranges:
            out_specs.append(pl.BlockSpec((1, slab, c1 - c0), out_map))
            out_shapes.append(jax.ShapeDtypeStruct((k, slab, c1 - c0), BF16))
    return ins, in_specs, out_specs, out_shapes


def _cast_slabs(jobs_cols, in_refs, out_refs):
    outs = iter(out_refs)
    for ref, col_ranges in zip(in_refs, jobs_cols):
        for c0, c1 in col_ranges:
            next(outs)[0] = ref[0, :, c0:c1].astype(BF16)


def _cast_results(jobs, outs):
    outs = iter(outs)
    return [[next(outs).reshape(w.shape[1], c1 - c0) for c0, c1 in col_ranges]
            for w, _, col_ranges in jobs]


def _ab_in_kernel(x_ref, g_ref, wqkv_ref, wf_ref, wconv_ref, cw_ref,
                  qkv_ref, f_ref, b_ref, zbuf, *, tm, seq):
    i = pl.program_id(0)

    @pl.when((i * tm) % seq == 0)
    def _():
        zbuf[0:SUBLANES, :] = jnp.zeros((SUBLANES, CONV_WIDTH), F32)

    h = _rms_norm(x_ref[...], g_ref[...]).astype(BF16)
    u = jnp.dot(h, wconv_ref[...], preferred_element_type=F32)
    u_b = u[:, :CONV_WIDTH]
    z = u[:, CONV_WIDTH:2 * CONV_WIDTH] * u[:, 2 * CONV_WIDTH:]
    zbuf[SUBLANES:SUBLANES + tm, :] = z
    cw = cw_ref[...]
    y = (cw[0:1, :] * zbuf[SUBLANES - 2:SUBLANES - 2 + tm, :]
         + cw[1:2, :] * zbuf[SUBLANES - 1:SUBLANES - 1 + tm, :]
         + cw[2:3, :] * z)
    b_ref[...] = (u_b * y).astype(BF16)
    zbuf[0:SUBLANES, :] = zbuf[tm:tm + SUBLANES, :]

    f = jnp.dot(h, wf_ref[...], preferred_element_type=F32)
    f_ref[...] = jnp.transpose(f)[0:f_ref.shape[0], :]
    qkv = jnp.dot(h, wqkv_ref[...], preferred_element_type=F32)
    qkv_ref[:, :FOX_WIDTH] = (qkv[:, :FOX_WIDTH] * FOX_Q_SCALE).astype(BF16)
    qkv_ref[:, FOX_WIDTH:] = qkv[:, FOX_WIDTH:].astype(BF16)


def _ab_in(x2d, g, wqkv, wf, wconv, cw, *, seq, tm=1024):
    t = x2d.shape[0]
    kern = functools.partial(_ab_in_kernel, tm=tm, seq=seq)
    return pl.pallas_call(
        kern,
        grid=(t // tm,),
        in_specs=[
            pl.BlockSpec((tm, D_MODEL), lambda i: (i, 0)),
            _resident((1, D_MODEL)),
            _resident(wqkv.shape),
            _resident(wf.shape),
            _resident(wconv.shape),
            _resident(cw.shape),
        ],
        out_specs=[
            pl.BlockSpec((tm, 3 * FOX_WIDTH), lambda i: (i, 0)),
            pl.BlockSpec((FOX_HEADS, tm), lambda i: (0, i)),
            pl.BlockSpec((tm, CONV_WIDTH), lambda i: (i, 0)),
        ],
        out_shape=[
            jax.ShapeDtypeStruct((t, 3 * FOX_WIDTH), BF16),
            jax.ShapeDtypeStruct((FOX_HEADS, t), F32),
            jax.ShapeDtypeStruct((t, CONV_WIDTH), BF16),
        ],
        scratch_shapes=[pltpu.VMEM((tm + SUBLANES, CONV_WIDTH), F32)],
        compiler_params=_params("arbitrary"),
        name="ab_in",
    )(x2d, g, wqkv, wf, wconv, cw)


def _fox_cumsum_kernel(f_ref, bias_ref, c_ref, *, seq):
    heads, total = c_ref.shape
    x = f_ref[...] + bias_ref[...]
    ls = jnp.minimum(x, 0.0) - jnp.log(1.0 + jnp.exp(-jnp.abs(x)))
    lane = lax.broadcasted_iota(jnp.int32, (heads, LANES), 1)
    blocks = [ls[:, j * LANES:(j + 1) * LANES] for j in range(total // LANES)]
    shift = 1
    while shift < LANES:
        blocks = [blk + jnp.where(lane >= shift, pltpu.roll(blk, shift, 1), 0.0) for blk in blocks]
        shift *= 2
    per_seq = seq // LANES
    for j, blk in enumerate(blocks):
        if j % per_seq:
            blk = blk + carry
        c_ref[:, j * LANES:(j + 1) * LANES] = blk * LOG2E
        carry = blk[:, LANES - 1:LANES]


def _fox_cumsum(f_rows, bias_col, *, seq):
    heads, total = bias_col.shape[0], f_rows.shape[1]
    return pl.pallas_call(
        functools.partial(_fox_cumsum_kernel, seq=seq),
        grid=(1,),
        in_specs=[pl.BlockSpec((heads, total), lambda i: (0, 0)), _resident(bias_col.shape)],
        out_specs=pl.BlockSpec((heads, total), lambda i: (0, 0)),
        out_shape=jax.ShapeDtypeStruct((heads, total), F32),
        compiler_params=_params("arbitrary"),
        name="fox_cumsum",
    )(f_rows, bias_col)


def _fox_attn_kernel(q_ref, k_ref, v_ref, crow_ref, *rest, tq, cast_cols):
    cast_in, o_ref, cast_out = rest[:len(cast_cols)], rest[len(cast_cols)], rest[len(cast_cols) + 1:]
    _cast_slabs(cast_cols, cast_in, cast_out)
    nblk = k_ref.shape[1] // tq
    lane = lax.broadcasted_iota(jnp.int32, (1, LANES), 1)
    row = lax.broadcasted_iota(jnp.int32, (tq, tq), 0)
    col = lax.broadcasted_iota(jnp.int32, (tq, tq), 1)
    causal = row >= col
    in_head = [(lane >= hh * FOX_HEAD_DIM) & (lane < (hh + 1) * FOX_HEAD_DIM) for hh in range(2)]
    rows = lambda ref, c: ref[0, c * tq:(c + 1) * tq, :]

    for first in range(nblk // 2):
        probs = [(c, hh) for c in (first, nblk - 1 - first) for hh in range(2)]
        s = []
        for c, hh in probs:
            q = rows(q_ref, c)
            qh = jnp.where(in_head[hh], q, jnp.zeros_like(q))
            sj = []
            for j in range(c + 1):
                sh = lax.dot_general(qh, rows(k_ref, j), NT_DIMS, preferred_element_type=F32)
                sh = sh - crow_ref[hh, 0, j:j + 1, :]
                sj.append(jnp.where(causal, sh, NEG_BIG) if j == c else sh)
            s.append(sj)
        shift = []
        for (c, hh), sj in zip(probs, s):
            mx = None
            for blk in sj:
                for cb in range(tq // LANES):
                    part = blk[:, cb * LANES:(cb + 1) * LANES]
                    mx = part if mx is None else jnp.maximum(mx, part)
            ct = jnp.transpose(jnp.broadcast_to(crow_ref[hh, 0, c:c + 1, :], (SUBLANES, tq)))[:, 0:1]
            m = jnp.max(mx, axis=-1, keepdims=True) + ct
            shift.append(m - ct)
        outs = []
        for (c, hh), sj, sh in zip(probs, s, shift):
            acc = None
            for j in range(c + 1):
                v = rows(v_ref, j)
                v_aug = jnp.concatenate([v, jnp.ones_like(v)], axis=1)
                d = jnp.dot(jnp.exp2(sj[j] - sh).astype(BF16), v_aug, preferred_element_type=F32)
                acc = d if acc is None else acc + d
            outs.append(acc[:, :LANES] / acc[:, LANES:])
        for n, c in enumerate((first, nblk - 1 - first)):
            o_ref[0, c * tq:(c + 1) * tq, :] = jnp.where(
                lane < FOX_HEAD_DIM, outs[2 * n], outs[2 * n + 1]).astype(o_ref.dtype)


def _fox_attn(qkv, c_row, cast_jobs, *, tq):
    bsz, seq, _ = qkv.shape
    n_pairs = FOX_WIDTH // LANES
    nblk = seq // tq
    seq_block = lambda col0: pl.BlockSpec((1, seq, LANES), lambda b, p: (b, 0, col0 + p))
    c_ins, c_in_specs, c_out_specs, c_out_shapes = _cast_operands(
        cast_jobs, bsz * n_pairs, lambda b, p: b * n_pairs + p)
    outs = pl.pallas_call(
        functools.partial(_fox_attn_kernel, tq=tq, cast_cols=[job[-1] for job in cast_jobs]),
        grid=(bsz, n_pairs),
        in_specs=[
            seq_block(0),
            seq_block(n_pairs),
            seq_block(2 * n_pairs),
            pl.BlockSpec((2, 1, nblk, tq), lambda b, p: (p, b, 0, 0)),
        ] + c_in_specs,
        out_specs=[seq_block(0)] + c_out_specs,
        out_shape=[jax.ShapeDtypeStruct((bsz, seq, FOX_WIDTH), BF16)] + c_out_shapes,
        compiler_params=_params("parallel", "parallel"),
        name="fox_attn",
    )(qkv, qkv, qkv, c_row, *c_ins)
    return outs[0], _cast_results(cast_jobs, outs[1:])


def _post_kernel(*refs, n_parts, final):
    parts = refs[:n_parts]
    x_ref, wo_ref, g_ref, w1_ref, w2_ref, gf_ref, o_ref = refs[n_parts:]
    hidden = w2_ref.shape[0]
    mix = None
    off = 0
    for p_ref in parts:
        width = p_ref.shape[1]
        d = jnp.dot(p_ref[...], wo_ref[off:off + width, :], preferred_element_type=F32)
        mix = d if mix is None else mix + d
        off += width
    x1 = x_ref[...] + mix
    h = _rms_norm(x1, g_ref[...]).astype(BF16)
    gu = jnp.dot(h, w1_ref[...], preferred_element_type=F32)
    gate = gu[:, :hidden]
    act = (_silu(gate) * gu[:, hidden:]).astype(BF16)
    x2 = x1 + jnp.dot(act, w2_ref[...], preferred_element_type=F32)
    if final:
        x2 = _rms_norm(x2, gf_ref[...])
    o_ref[...] = x2


def _post(parts, x2d, wo, g, w1, w2, gf, *, final, tm=512):
    t = x2d.shape[0]
    kern = functools.partial(_post_kernel, n_parts=len(parts), final=final)
    in_specs = [pl.BlockSpec((tm, p.shape[1]), lambda i: (i, 0)) for p in parts]
    in_specs += [
        pl.BlockSpec((tm, D_MODEL), lambda i: (i, 0)),
        _resident(wo.shape),
        _resident((1, D_MODEL)),
        _resident(w1.shape),
        _resident(w2.shape),
        _resident((1, D_MODEL)),
    ]
    return pl.pallas_call(
        kern,
        grid=(t // tm,),
        in_specs=in_specs,
        out_specs=pl.BlockSpec((tm, D_MODEL), lambda i: (i, 0)),
        out_shape=jax.ShapeDtypeStruct((t, D_MODEL), F32),
        compiler_params=_params("parallel"),
        name="post",
    )(*parts, x2d, wo, g, w1, w2, gf)


def _hgrn_levels():
    h, out = HGRN_CHUNK // 2, []
    while h >= 1:
        out.append(h)
        h //= 2
    return out


def _hgrn_cum_table():
    t = np.arange(HGRN_CHUNK)
    m = (t[None, :] <= t[:, None]).astype(np.float32)
    return np.concatenate([m, m, m], axis=1)


def _hgrn_masks():
    c = HGRN_CHUNK
    levels = _hgrn_levels()
    row = lax.broadcasted_iota(jnp.int32, (c, HGRN_HEAD_DIM), 0)
    r2 = lax.broadcasted_iota(jnp.int32, (c, c), 0)
    c2 = lax.broadcasted_iota(jnp.int32, (c, c), 1)
    upper = [(row & h) != 0 for h in levels]
    pair = [((r2 // (2 * h)) == (c2 // (2 * h))) & ((r2 & h) != 0) & ((c2 & h) == 0)
            for h in levels]
    sign = [jnp.where(u, 1.0, -1.0) for u in upper]
    return row, upper, pair, sign, r2 == c2


def _hgrn_block(q16, f_logit, v, g16, lb, hn, cum, st, masks):
    c = HGRN_CHUNK
    levels = _hgrn_levels()
    row, upper, pair, sign, diag = masks

    n = q16.shape[0] // c
    chunk = lambda arr, i: arr[i * c:(i + 1) * c]
    q = q16.astype(F32)
    half_gap = 0.5 * (1.0 - lb)
    swing = half_gap * jnp.tanh(0.5 * f_logit)
    f = (1.0 - half_gap) + swing
    k = half_gap - swing
    lg = jnp.log2(f)
    hi = lg.astype(BF16)
    r1 = lg - hi.astype(F32)
    mid = r1.astype(BF16)
    lo = (r1 - mid.astype(F32)).astype(BF16)
    b = [jnp.dot(cum, jnp.concatenate([chunk(hi, i), chunk(mid, i), chunk(lo, i)], axis=0),
                 preferred_element_type=F32) for i in range(n)]

    qk = jnp.sum(q * k, axis=-1, keepdims=True)

    a = [None] * n
    o_inter = []
    for li, h in enumerate(levels):
        xs = []
        for i in range(n):
            qi, ki, bi = chunk(q, i), chunk(k, i), b[i]
            if h == 1:
                x = jnp.where(upper[li], qi * chunk(f, i), ki)
            else:
                if h >= SUBLANES:
                    ref = jnp.concatenate(
                        [jnp.broadcast_to(bi[base + h - 1:base + h, :], (2 * h, HGRN_HEAD_DIM))
                         for base in range(0, c, 2 * h)], axis=0)
                else:
                    b3 = bi.reshape(c // SUBLANES, SUBLANES, HGRN_HEAD_DIM)
                    refs = [jnp.broadcast_to(b3[:, base + h - 1:base + h, :], b3.shape).reshape(bi.shape)
                            for base in range(0, SUBLANES, 2 * h)]
                    ref = refs[0]
                    for m in range(1, len(refs)):
                        ref = jnp.where((row % SUBLANES) >= m * 2 * h, refs[m], ref)
                x = jnp.where(upper[li], qi, ki) * jnp.exp2((bi - ref) * sign[li])
            xs.append(x.astype(BF16))
        ps = [lax.dot_general(x, x, NT_DIMS, preferred_element_type=F32) for x in xs]
        a = [jnp.where(pair[li], ps[i], 0.0 if a[i] is None else a[i]) for i in range(n)]

        for i in range(li * n // len(levels), (li + 1) * n // len(levels)):
            b_last = b[i][c - 1:c, :]
            kv = lax.dot_general(chunk(v, i), (chunk(k, i) * jnp.exp2(b_last - b[i])).astype(BF16),
                                 TN_DIMS, preferred_element_type=F32)
            o_inter.append(lax.dot_general((chunk(q, i) * jnp.exp2(b[i])).astype(BF16),
                                           st.astype(BF16), NT_DIMS, preferred_element_type=F32))
            st = st * jnp.exp2(b_last) + kv
    a = [jnp.where(diag, chunk(qk, i), a[i]) for i in range(n)]
    o_intra = [jnp.dot(a[i].astype(BF16), chunk(v, i), preferred_element_type=F32) for i in range(n)]

    o = jnp.concatenate([o_inter[i] + o_intra[i] for i in range(n)], axis=0)
    o = o * lax.rsqrt(jnp.mean(o * o, axis=-1, keepdims=True) + EPS) * hn
    g = g16.astype(F32)
    return (o * _silu(g)).astype(BF16), st


def _c_in_kernel(x_ref, g_ref, w_ref, o_ref, f_ref):
    h = _rms_norm(x_ref[...], g_ref[...]).astype(BF16)
    pr = jnp.dot(h, w_ref[...], preferred_element_type=F32)
    o_ref[:, :D_MODEL] = pr[:, :D_MODEL].astype(BF16)
    o_ref[:, D_MODEL:] = pr[:, 2 * D_MODEL:].astype(BF16)
    f_ref[...] = pr[:, D_MODEL:2 * D_MODEL]


def _c_in(x2d, g, w, *, tm=1024):
    t = x2d.shape[0]
    return pl.pallas_call(
        _c_in_kernel,
        grid=(t // tm,),
        in_specs=[
            pl.BlockSpec((tm, D_MODEL), lambda i: (i, 0)),
            _resident((1, D_MODEL)),
            _resident(w.shape),
        ],
        out_specs=[
            pl.BlockSpec((tm, 3 * D_MODEL), lambda i: (i, 0)),
            pl.BlockSpec((tm, D_MODEL), lambda i: (i, 0)),
        ],
        out_shape=[
            jax.ShapeDtypeStruct((t, 3 * D_MODEL), BF16),
            jax.ShapeDtypeStruct((t, D_MODEL), F32),
        ],
        compiler_params=_params("parallel"),
        name="c_in",
    )(x2d, g, w)


HGRN_HEADS_PER_STEP = 2


def _hgrn_kernel(q_ref, f_ref, i_ref, g_ref, lbraw_ref, hn_ref, cum_ref, *rest,
                 layer_idx, cast_cols):
    n_cast = len(cast_cols)
    cast_in, o_ref, cast_out, state = rest[:n_cast], rest[n_cast], rest[n_cast + 1:-1], rest[-1]
    _cast_slabs(cast_cols, cast_in, cast_out)

    @pl.when(pl.program_id(2) == 0)
    def _():
        state[...] = jnp.zeros_like(state)

    raw = lbraw_ref[...]
    e = jnp.exp(raw - jnp.max(raw, axis=0, keepdims=True))
    sm = e / jnp.sum(e, axis=0, keepdims=True)
    lb = jnp.sum(sm[0:layer_idx + 1, :], axis=0, keepdims=True) - sm[0:1, :]
    hn = hn_ref[...]
    masks = _hgrn_masks()
    for n in range(HGRN_HEADS_PER_STEP):
        hs = slice(n * HGRN_HEAD_DIM, (n + 1) * HGRN_HEAD_DIM)
        o, st = _hgrn_block(q_ref[0, :, hs], f_ref[0, :, hs], i_ref[0, :, hs], g_ref[0, :, hs],
                            lb[:, hs], hn[:, hs], cum_ref[...], state[n], masks)
        state[n] = st
        o_ref[0, :, hs] = o


def _hgrn(proj, f_logit, lb_raw, head_norm, cast_jobs, *, layer_idx, blk=2048):
    bsz, seq, _ = f_logit.shape
    hd = HGRN_HEAD_DIM
    per = HGRN_HEADS_PER_STEP
    width = per * hd
    ng = HGRN_HEADS // per
    n_l = seq // blk
    cum = jnp.asarray(_hgrn_cum_table(), BF16)
    c_ins, c_in_specs, c_out_specs, c_out_shapes = _cast_operands(
        cast_jobs, bsz * ng * n_l, lambda b, h, l: (b * ng + h) * n_l + l)
    outs = pl.pallas_call(
        functools.partial(_hgrn_kernel, layer_idx=layer_idx,
                          cast_cols=[job[-1] for job in cast_jobs]),
        grid=(bsz, ng, n_l),
        in_specs=[
            pl.BlockSpec((1, blk, width), lambda b, h, l: (b, l, h)),
            pl.BlockSpec((1, blk, width), lambda b, h, l: (b, l, h)),
            pl.BlockSpec((1, blk, width), lambda b, h, l: (b, l, ng + h)),
            pl.BlockSpec((1, blk, width), lambda b, h, l: (b, l, 2 * ng + h)),
            pl.BlockSpec((lb_raw.shape[0], width), lambda b, h, l: (0, h)),
            pl.BlockSpec((1, width), lambda b, h, l: (0, h)),
            _resident(cum.shape),
        ] + c_in_specs,
        out_specs=[pl.BlockSpec((1, blk, width), lambda b, h, l: (b, l, h))] + c_out_specs,
        out_shape=[jax.ShapeDtypeStruct((bsz, seq, D_MODEL), BF16)] + c_out_shapes,
        scratch_shapes=[pltpu.VMEM((per, hd, hd), F32)],
        compiler_params=_params("parallel", "parallel", "arbitrary"),
        name="hgrn",
    )(proj, f_logit, proj, proj, lb_raw, head_norm, cum, *c_ins)
    return outs[0], _cast_results(cast_jobs, outs[1:])


def kernel(x, norm_mix, norm_ffn, final_norm, ab_w_in, fox_f_bias, conv_w, ab_w_out,
           c_w_in, c_lower_bounds, c_head_norm, c_w_out, ffn_w_in, ffn_w_out):
    bsz, seq, d = x.shape
    depth = norm_mix.shape[0]
    t = bsz * seq
    x2d = x.reshape(t, d)
    gf = final_norm.reshape(1, d)
    q_end = 3 * FOX_WIDTH
    f_end = q_end + FOX_HEADS

    ab_cols = [(0, q_end), (q_end, q_end + LANES), (f_end, f_end + 3 * CONV_WIDTH)]
    whole = lambda w, layer: (w, layer, [(0, w.shape[2])])

    def in_proj_job(layer):
        j = layer // 2
        return (ab_w_in, j, ab_cols) if layer % 2 == 0 else whole(c_w_in, j)

    w0, j0, cols0 = in_proj_job(0)
    in_w = [w0[j0, :, c0:c1].astype(BF16) for c0, c1 in cols0]

    for layer in range(depth):
        j = layer // 2
        g_mix = norm_mix[layer].reshape(1, d)
        jobs = [whole(ab_w_out if layer % 2 == 0 else c_w_out, j),
                whole(ffn_w_in, layer), whole(ffn_w_out, layer)]
        if layer + 1 < depth:
            jobs.append(in_proj_job(layer + 1))
        if layer % 2 == 0:
            wqkv, wf, wconv = in_w
            qkv, f_rows, b_out = _ab_in(x2d, g_mix, wqkv, wf, wconv, conv_w[j], seq=seq)
            c = _fox_cumsum(f_rows, fox_f_bias[j].reshape(FOX_HEADS, 1), seq=seq)
            tq = FOX_Q_TILE
            c_row = c.reshape(FOX_HEADS, bsz, seq // tq, tq)
            a_out, cast = _fox_attn(qkv.reshape(bsz, seq, q_end), c_row, jobs, tq=tq)
            parts = [a_out.reshape(t, FOX_WIDTH), b_out]
        else:
            proj, f_logit = _c_in(x2d, g_mix, in_w[0])
            o, cast = _hgrn(proj.reshape(bsz, seq, 3 * d), f_logit.reshape(bsz, seq, d),
                            c_lower_bounds, c_head_norm[j].reshape(1, d), jobs, layer_idx=j)
            parts = [o.reshape(t, d)]
        (wo,), (w1,), (w2,) = cast[:3]
        in_w = cast[3] if layer + 1 < depth else None
        x2d = _post(parts, x2d, wo, norm_ffn[layer].reshape(1, d), w1, w2, gf,
                    final=(layer == depth - 1))
    return x2d.reshape(bsz, seq, d)
```

```python
import functools

import jax
import jax.numpy as jnp
import numpy as np
from jax import lax
from jax.experimental import pallas as pl
from jax.experimental.pallas import tpu as pltpu

D_MODEL = 1024
EPS = 1e-6
NEG_BIG = -1e30

FOX_HEADS = 8
FOX_HEAD_DIM = 64
FOX_WIDTH = FOX_HEADS * FOX_HEAD_DIM
CONV_WIDTH = D_MODEL - FOX_WIDTH
CONV_TAPS = 3
LOG2E = 1.4426950408889634
FOX_Q_SCALE = LOG2E * FOX_HEAD_DIM ** -0.5
FOX_Q_TILE = 256
HGRN_HEADS = 8
HGRN_HEAD_DIM = D_MODEL // HGRN_HEADS
HGRN_CHUNK = 64

LANES = 128
SUBLANES = 8
VMEM_LIMIT = 56 * 1024 * 1024

BF16 = jnp.bfloat16
F32 = jnp.float32

NT_DIMS = (((1,), (1,)), ((), ()))
TN_DIMS = (((0,), (0,)), ((), ()))


def _resident(shape):
    zeros = (0,) * len(shape)
    return pl.BlockSpec(shape, lambda *_: zeros, pipeline_mode=pl.Buffered(1))


def _params(*sem):
    return pltpu.CompilerParams(dimension_semantics=sem, vmem_limit_bytes=VMEM_LIMIT)


def _rms_norm(x, g):
    return x * lax.rsqrt(jnp.mean(x * x, axis=-1, keepdims=True) + EPS) * g


def _silu(x):
    half = 0.5 * x
    return half * (jnp.tanh(half) + 1.0)


BF16_SUBLANES = 2 * SUBLANES


def _cast_slabs_per_job(rows, n_steps):
    return max(k for k in range(1, n_steps + 1)
               if rows % k == 0 and (rows // k) % BF16_SUBLANES == 0)


def _cast_operands(jobs, n_steps, step_of):
    ins, in_specs, out_specs, out_shapes = [], [], [], []
    for w, layer, col_ranges in jobs:
        layers, rows, cols = w.shape
        k = _cast_slabs_per_job(rows, n_steps)
        slab = rows // k
        in_map = lambda *g, k=k, layer=layer: (layer * k + jnp.minimum(step_of(*g), k - 1), 0, 0)
        out_map = lambda *g, k=k: (jnp.minimum(step_of(*g), k - 1), 0, 0)
        ins.append(w.reshape(layers * k, slab, cols))
        in_specs.append(pl.BlockSpec((1, slab, cols), in_map))
        for c0, c1 in col_ranges:
            out_specs.append(pl.BlockSpec((1, slab, c1 - c0), out_map))
            out_shapes.append(jax.ShapeDtypeStruct((k, slab, c1 - c0), BF16))
    return ins, in_specs, out_specs, out_shapes


def _cast_slabs(jobs_cols, in_refs, out_refs):
    outs = iter(out_refs)
    for ref, col_ranges in zip(in_refs, jobs_cols):
        for c0, c1 in col_ranges:
            next(outs)[0] = ref[0, :, c0:c1].astype(BF16)


def _cast_results(jobs, outs):
    outs = iter(outs)
    return [[next(outs).reshape(w.shape[1], c1 - c0) for c0, c1 in col_ranges]
            for w, _, col_ranges in jobs]


def _ab_in_kernel(x_ref, g_ref, wqkv_ref, wf_ref, wconv_ref, cw_ref,
                  qkv_ref, f_ref, b_ref, zbuf, *, tm, seq):
    i = pl.program_id(0)

    @pl.when((i * tm) % seq == 0)
    def _():
        zbuf[0:SUBLANES, :] = jnp.zeros((SUBLANES, CONV_WIDTH), F32)

    h = _rms_norm(x_ref[...], g_ref[...]).astype(BF16)
    u = jnp.dot(h, wconv_ref[...], preferred_element_type=F32)
    u_b = u[:, :CONV_WIDTH]
    z = u[:, CONV_WIDTH:2 * CONV_WIDTH] * u[:, 2 * CONV_WIDTH:]
    zbuf[SUBLANES:SUBLANES + tm, :] = z
    cw = cw_ref[...]
    y = (cw[0:1, :] * zbuf[SUBLANES - 2:SUBLANES - 2 + tm, :]
         + cw[1:2, :] * zbuf[SUBLANES - 1:SUBLANES - 1 + tm, :]
         + cw[2:3, :] * z)
    b_ref[...] = (u_b * y).astype(BF16)
    zbuf[0:SUBLANES, :] = zbuf[tm:tm + SUBLANES, :]

    f = jnp.dot(h, wf_ref[...], preferred_element_type=F32)
    f_ref[...] = jnp.transpose(f)[0:f_ref.shape[0], :]
    qkv = jnp.dot(h, wqkv_ref[...], preferred_element_type=F32)
    qkv_ref[:, :FOX_WIDTH] = (qkv[:, :FOX_WIDTH] * FOX_Q_SCALE).astype(BF16)
    qkv_ref[:, FOX_WIDTH:] = qkv[:, FOX_WIDTH:].astype(BF16)


def _ab_in(x2d, g, wqkv, wf, wconv, cw, *, seq, tm=1024):
    t = x2d.shape[0]
    kern = functools.partial(_ab_in_kernel, tm=tm, seq=seq)
    return pl.pallas_call(
        kern,
        grid=(t // tm,),
        in_specs=[
            pl.BlockSpec((tm, D_MODEL), lambda i: (i, 0)),
            _resident((1, D_MODEL)),
            _resident(wqkv.shape),
            _resident(wf.shape),
            _resident(wconv.shape),
            _resident(cw.shape),
        ],
        out_specs=[
            pl.BlockSpec((tm, 3 * FOX_WIDTH), lambda i: (i, 0)),
            pl.BlockSpec((FOX_HEADS, tm), lambda i: (0, i)),
            pl.BlockSpec((tm, CONV_WIDTH), lambda i: (i, 0)),
        ],
        out_shape=[
            jax.ShapeDtypeStruct((t, 3 * FOX_WIDTH), BF16),
            jax.ShapeDtypeStruct((FOX_HEADS, t), F32),
            jax.ShapeDtypeStruct((t, CONV_WIDTH), BF16),
        ],
        scratch_shapes=[pltpu.VMEM((tm + SUBLANES, CONV_WIDTH), F32)],
        compiler_params=_params("arbitrary"),
        name="ab_in",
    )(x2d, g, wqkv, wf, wconv, cw)


def _fox_cumsum_kernel(f_ref, bias_ref, c_ref, *, seq):
    heads, total = c_ref.shape
    x = f_ref[...] + bias_ref[...]
    ls = jnp.minimum(x, 0.0) - jnp.log(1.0 + jnp.exp(-jnp.abs(x)))
    lane = lax.broadcasted_iota(jnp.int32, (heads, LANES), 1)
    blocks = [ls[:, j * LANES:(j + 1) * LANES] for j in range(total // LANES)]
    shift = 1
    while shift < LANES:
        blocks = [blk + jnp.where(lane >= shift, pltpu.roll(blk, shift, 1), 0.0) for blk in blocks]
        shift *= 2
    per_seq = seq // LANES
    for j, blk in enumerate(blocks):
        if j % per_seq:
            blk = blk + carry
        c_ref[:, j * LANES:(j + 1) * LANES] = blk * LOG2E
        carry = blk[:, LANES - 1:LANES]


def _fox_cumsum(f_rows, bias_col, *, seq):
    heads, total = bias_col.shape[0], f_rows.shape[1]
    return pl.pallas_call(
        functools.partial(_fox_cumsum_kernel, seq=seq),
        grid=(1,),
        in_specs=[pl.BlockSpec((heads, total), lambda i: (0, 0)), _resident(bias_col.shape)],
        out_specs=pl.BlockSpec((heads, total), lambda i: (0, 0)),
        out_shape=jax.ShapeDtypeStruct((heads, total), F32),
        compiler_params=_params("arbitrary"),
        name="fox_cumsum",
    )(f_rows, bias_col)


def _fox_attn_kernel(q_ref, k_ref, v_ref, crow_ref, *rest, tq, cast_cols):
    cast_in, o_ref, cast_out = rest[:len(cast_cols)], rest[len(cast_cols)], rest[len(cast_cols) + 1:]
    _cast_slabs(cast_cols, cast_in, cast_out)
    nblk = k_ref.shape[1] // tq
    lane = lax.broadcasted_iota(jnp.int32, (1, LANES), 1)
    row = lax.broadcasted_iota(jnp.int32, (tq, tq), 0)
    col = lax.broadcasted_iota(jnp.int32, (tq, tq), 1)
    causal = row >= col
    in_head = [(lane >= hh * FOX_HEAD_DIM) & (lane < (hh + 1) * FOX_HEAD_DIM) for hh in range(2)]
    rows = lambda ref, c: ref[0, c * tq:(c + 1) * tq, :]

    for first in range(nblk // 2):
        probs = [(c, hh) for c in (first, nblk - 1 - first) for hh in range(2)]
        s = []
        for c, hh in probs:
            q = rows(q_ref, c)
            qh = jnp.where(in_head[hh], q, jnp.zeros_like(q))
            sj = []
            for j in range(c + 1):
                sh = lax.dot_general(qh, rows(k_ref, j), NT_DIMS, preferred_element_type=F32)
                sh = sh - crow_ref[hh, 0, j:j + 1, :]
                sj.append(jnp.where(causal, sh, NEG_BIG) if j == c else sh)
            s.append(sj)
        shift = []
        for (c, hh), sj in zip(probs, s):
            mx = None
            for blk in sj:
                for cb in range(tq // LANES):
                    part = blk[:, cb * LANES:(cb + 1) * LANES]
                    mx = part if mx is None else jnp.maximum(mx, part)
            ct = jnp.transpose(jnp.broadcast_to(crow_ref[hh, 0, c:c + 1, :], (SUBLANES, tq)))[:, 0:1]
            m = jnp.max(mx, axis=-1, keepdims=True) + ct
            shift.append(m - ct)
        outs = []
        for (c, hh), sj, sh in zip(probs, s, shift):
            acc = None
            for j in range(c + 1):
                v = rows(v_ref, j)
                v_aug = jnp.concatenate([v, jnp.ones_like(v)], axis=1)
                d = jnp.dot(jnp.exp2(sj[j] - sh).astype(BF16), v_aug, preferred_element_type=F32)
                acc = d if acc is None else acc + d
            outs.append(acc[:, :LANES] / acc[:, LANES:])
        for n, c in enumerate((first, nblk - 1 - first)):
            o_ref[0, c * tq:(c + 1) * tq, :] = jnp.where(
                lane < FOX_HEAD_DIM, outs[2 * n], outs[2 * n + 1]).astype(o_ref.dtype)


def _fox_attn(qkv, c_row, cast_jobs, *, tq):
    bsz, seq, _ = qkv.shape
    n_pairs = FOX_WIDTH // LANES
    nblk = seq // tq
    seq_block = lambda col0: pl.BlockSpec((1, seq, LANES), lambda b, p: (b, 0, col0 + p))
    c_ins, c_in_specs, c_out_specs, c_out_shapes = _cast_operands(
        cast_jobs, bsz * n_pairs, lambda b, p: b * n_pairs + p)
    outs = pl.pallas_call(
        functools.partial(_fox_attn_kernel, tq=tq, cast_cols=[job[-1] for job in cast_jobs]),
        grid=(bsz, n_pairs),
        in_specs=[
            seq_block(0),
            seq_block(n_pairs),
            seq_block(2 * n_pairs),
            pl.BlockSpec((2, 1, nblk, tq), lambda b, p: (p, b, 0, 0)),
        ] + c_in_specs,
        out_specs=[seq_block(0)] + c_out_specs,
        out_shape=[jax.ShapeDtypeStruct((bsz, seq, FOX_WIDTH), BF16)] + c_out_shapes,
        compiler_params=_params("parallel", "parallel"),
        name="fox_attn",
    )(qkv, qkv, qkv, c_row, *c_ins)
    return outs[0], _cast_results(cast_jobs, outs[1:])


def _post_kernel(*refs, n_parts, final):
    parts = refs[:n_parts]
    x_ref, wo_ref, g_ref, w1_ref, w2_ref, gf_ref, o_ref = refs[n_parts:]
    hidden = w2_ref.shape[0]
    mix = None
    off = 0
    for p_ref in parts:
        width = p_ref.shape[1]
        d = jnp.dot(p_ref[...], wo_ref[off:off + width, :], preferred_element_type=F32)
        mix = d if mix is None else mix + d
        off += width
    x1 = x_ref[...] + mix
    h = _rms_norm(x1, g_ref[...]).astype(BF16)
    gu = jnp.dot(h, w1_ref[...], preferred_element_type=F32)
    gate = gu[:, :hidden]
    act = (_silu(gate) * gu[:, hidden:]).astype(BF16)
    x2 = x1 + jnp.dot(act, w2_ref[...], preferred_element_type=F32)
    if final:
        x2 = _rms_norm(x2, gf_ref[...])
    o_ref[...] = x2


def _post(parts, x2d, wo, g, w1, w2, gf, *, final, tm=512):
    t = x2d.shape[0]
    kern = functools.partial(_post_kernel, n_parts=len(parts), final=final)
    in_specs = [pl.BlockSpec((tm, p.shape[1]), lambda i: (i, 0)) for p in parts]
    in_specs += [
        pl.BlockSpec((tm, D_MODEL), lambda i: (i, 0)),
        _resident(wo.shape),
        _resident((1, D_MODEL)),
        _resident(w1.shape),
        _resident(w2.shape),
        _resident((1, D_MODEL)),
    ]
    return pl.pallas_call(
        kern,
        grid=(t // tm,),
        in_specs=in_specs,
        out_specs=pl.BlockSpec((tm, D_MODEL), lambda i: (i, 0)),
        out_shape=jax.ShapeDtypeStruct((t, D_MODEL), F32),
        compiler_params=_params("parallel"),
        name="post",
    )(*parts, x2d, wo, g, w1, w2, gf)


def _hgrn_levels():
    h, out = HGRN_CHUNK // 2, []
    while h >= 1:
        out.append(h)
        h //= 2
    return out


def _hgrn_cum_table():
    t = np.arange(HGRN_CHUNK)
    m = (t[None, :] <= t[:, None]).astype(np.float32)
    return np.concatenate([m, m, m], axis=1)


def _hgrn_masks():
    c = HGRN_CHUNK
    levels = _hgrn_levels()
    row = lax.broadcasted_iota(jnp.int32, (c, HGRN_HEAD_DIM), 0)
    r2 = lax.broadcasted_iota(jnp.int32, (c, c), 0)
    c2 = lax.broadcasted_iota(jnp.int32, (c, c), 1)
    upper = [(row & h) != 0 for h in levels]
    pair = [((r2 // (2 * h)) == (c2 // (2 * h))) & ((r2 & h) != 0) & ((c2 & h) == 0)
            for h in levels]
    sign = [jnp.where(u, 1.0, -1.0) for u in upper]
    return row, upper, pair, sign, r2 == c2


def _hgrn_block(q16, f_logit, v, g16, lb, hn, cum, st, masks):
    c = HGRN_CHUNK
    levels = _hgrn_levels()
    row, upper, pair, sign, diag = masks

    n = q16.shape[0] // c
    chunk = lambda arr, i: arr[i * c:(i + 1) * c]
    q = q16.astype(F32)
    sig = pl.reciprocal(1.0 + jnp.exp2(f_logit * (-LOG2E)), approx=True)
    f = lb + (1.0 - lb) * sig
    k = 1.0 - f
    lg = jnp.log2(f)
    hi = lg.astype(BF16)
    r1 = lg - hi.astype(F32)
    mid = r1.astype(BF16)
    lo = (r1 - mid.astype(F32)).astype(BF16)
    b = [jnp.dot(cum, jnp.concatenate([chunk(hi, i), chunk(mid, i), chunk(lo, i)], axis=0),
                 preferred_element_type=F32) for i in range(n)]

    qk = jnp.sum(q * k, axis=-1, keepdims=True)

    a = [None] * n
    o_inter = []
    for li, h in enumerate(levels):
        xs = []
        for i in range(n):
            qi, ki, bi = chunk(q, i), chunk(k, i), b[i]
            if h == 1:
                x = jnp.where(upper[li], qi * chunk(f, i), ki)
            else:
                if h >= SUBLANES:
                    ref = jnp.concatenate(
                        [jnp.broadcast_to(bi[base + h - 1:base + h, :], (2 * h, HGRN_HEAD_DIM))
                         for base in range(0, c, 2 * h)], axis=0)
                else:
                    b3 = bi.reshape(c // SUBLANES, SUBLANES, HGRN_HEAD_DIM)
                    refs = [jnp.broadcast_to(b3[:, base + h - 1:base + h, :], b3.shape).reshape(bi.shape)
                            for base in range(0, SUBLANES, 2 * h)]
                    ref = refs[0]
                    for m in range(1, len(refs)):
                        ref = jnp.where((row % SUBLANES) >= m * 2 * h, refs[m], ref)
                x = jnp.where(upper[li], qi, ki) * jnp.exp2((bi - ref) * sign[li])
            xs.append(x.astype(BF16))
        ps = [lax.dot_general(x, x, NT_DIMS, preferred_element_type=F32) for x in xs]
        a = [jnp.where(pair[li], ps[i], 0.0 if a[i] is None else a[i]) for i in range(n)]

        for i in range(li * n // len(levels), (li + 1) * n // len(levels)):
            b_last = b[i][c - 1:c, :]
            kv = lax.dot_general(chunk(v, i), (chunk(k, i) * jnp.exp2(b_last - b[i])).astype(BF16),
                                 TN_DIMS, preferred_element_type=F32)
            o_inter.append(lax.dot_general((chunk(q, i) * jnp.exp2(b[i])).astype(BF16),
                                           st.astype(BF16), NT_DIMS, preferred_element_type=F32))
            st = st * jnp.exp2(b_last) + kv
    a = [jnp.where(diag, chunk(qk, i), a[i]) for i in range(n)]
    o_intra = [jnp.dot(a[i].astype(BF16), chunk(v, i), preferred_element_type=F32) for i in range(n)]

    o = jnp.concatenate([o_inter[i] + o_intra[i] for i in range(n)], axis=0)
    o = o * lax.rsqrt(jnp.mean(o * o, axis=-1, keepdims=True) + EPS) * hn
    g = g16.astype(F32)
    return (o * _silu(g)).astype(BF16), st


def _c_in_kernel(x_ref, g_ref, w_ref, o_ref, f_ref):
    h = _rms_norm(x_ref[...], g_ref[...]).astype(BF16)
    pr = jnp.dot(h, w_ref[...], preferred_element_type=F32)
    o_ref[:, :D_MODEL] = pr[:, :D_MODEL].astype(BF16)
    o_ref[:, D_MODEL:] = pr[:, 2 * D_MODEL:].astype(BF16)
    f_ref[...] = pr[:, D_MODEL:2 * D_MODEL]


def _c_in(x2d, g, w, *, tm=1024):
    t = x2d.shape[0]
    return pl.pallas_call(
        _c_in_kernel,
        grid=(t // tm,),
        in_specs=[
            pl.BlockSpec((tm, D_MODEL), lambda i: (i, 0)),
            _resident((1, D_MODEL)),
            _resident(w.shape),
        ],
        out_specs=[
            pl.BlockSpec((tm, 3 * D_MODEL), lambda i: (i, 0)),
            pl.BlockSpec((tm, D_MODEL), lambda i: (i, 0)),
        ],
        out_shape=[
            jax.ShapeDtypeStruct((t, 3 * D_MODEL), BF16),
            jax.ShapeDtypeStruct((t, D_MODEL), F32),
        ],
        compiler_params=_params("parallel"),
        name="c_in",
    )(x2d, g, w)


HGRN_HEADS_PER_STEP = 2


def _hgrn_kernel(q_ref, f_ref, i_ref, g_ref, lbraw_ref, hn_ref, cum_ref, *rest,
                 layer_idx, cast_cols):
    n_cast = len(cast_cols)
    cast_in, o_ref, cast_out, state = rest[:n_cast], rest[n_cast], rest[n_cast + 1:-1], rest[-1]
    _cast_slabs(cast_cols, cast_in, cast_out)

    @pl.when(pl.program_id(2) == 0)
    def _():
        state[...] = jnp.zeros_like(state)

    raw = lbraw_ref[...]
    e = jnp.exp(raw - jnp.max(raw, axis=0, keepdims=True))
    sm = e / jnp.sum(e, axis=0, keepdims=True)
    lb = jnp.sum(sm[0:layer_idx + 1, :], axis=0, keepdims=True) - sm[0:1, :]
    hn = hn_ref[...]
    masks = _hgrn_masks()
    for n in range(HGRN_HEADS_PER_STEP):
        hs = slice(n * HGRN_HEAD_DIM, (n + 1) * HGRN_HEAD_DIM)
        o, st = _hgrn_block(q_ref[0, :, hs], f_ref[0, :, hs], i_ref[0, :, hs], g_ref[0, :, hs],
                            lb[:, hs], hn[:, hs], cum_ref[...], state[n], masks)
        state[n] = st
        o_ref[0, :, hs] = o


def _hgrn(proj, f_logit, lb_raw, head_norm, cast_jobs, *, layer_idx, blk=2048):
    bsz, seq, _ = f_logit.shape
    hd = HGRN_HEAD_DIM
    per = HGRN_HEADS_PER_STEP
    width = per * hd
    ng = HGRN_HEADS // per
    n_l = seq // blk
    cum = jnp.asarray(_hgrn_cum_table(), BF16)
    c_ins, c_in_specs, c_out_specs, c_out_shapes = _cast_operands(
        cast_jobs, bsz * ng * n_l, lambda b, h, l: (b * ng + h) * n_l + l)
    outs = pl.pallas_call(
        functools.partial(_hgrn_kernel, layer_idx=layer_idx,
                          cast_cols=[job[-1] for job in cast_jobs]),
        grid=(bsz, ng, n_l),
        in_specs=[
            pl.BlockSpec((1, blk, width), lambda b, h, l: (b, l, h)),
            pl.BlockSpec((1, blk, width), lambda b, h, l: (b, l, h)),
            pl.BlockSpec((1, blk, width), lambda b, h, l: (b, l, ng + h)),
            pl.BlockSpec((1, blk, width), lambda b, h, l: (b, l, 2 * ng + h)),
            pl.BlockSpec((lb_raw.shape[0], width), lambda b, h, l: (0, h)),
            pl.BlockSpec((1, width), lambda b, h, l: (0, h)),
            _resident(cum.shape),
        ] + c_in_specs,
        out_specs=[pl.BlockSpec((1, blk, width), lambda b, h, l: (b, l, h))] + c_out_specs,
        out_shape=[jax.ShapeDtypeStruct((bsz, seq, D_MODEL), BF16)] + c_out_shapes,
        scratch_shapes=[pltpu.VMEM((per, hd, hd), F32)],
        compiler_params=_params("parallel", "parallel", "arbitrary"),
        name="hgrn",
    )(proj, f_logit, proj, proj, lb_raw, head_norm, cum, *c_ins)
    return outs[0], _cast_results(cast_jobs, outs[1:])


def kernel(x, norm_mix, norm_ffn, final_norm, ab_w_in, fox_f_bias, conv_w, ab_w_out,
           c_w_in, c_lower_bounds, c_head_norm, c_w_out, ffn_w_in, ffn_w_out):
    bsz, seq, d = x.shape
    depth = norm_mix.shape[0]
    t = bsz * seq
    x2d = x.reshape(t, d)
    gf = final_norm.reshape(1, d)
    q_end = 3 * FOX_WIDTH
    f_end = q_end + FOX_HEADS

    ab_cols = [(0, q_end), (q_end, q_end + LANES), (f_end, f_end + 3 * CONV_WIDTH)]
    whole = lambda w, layer: (w, layer, [(0, w.shape[2])])

    def in_proj_job(layer):
        j = layer // 2
        return (ab_w_in, j, ab_cols) if layer % 2 == 0 else whole(c_w_in, j)

    w0, j0, cols0 = in_proj_job(0)
    in_w = [w0[j0, :, c0:c1].astype(BF16) for c0, c1 in cols0]

    for layer in range(depth):
        j = layer // 2
        g_mix = norm_mix[layer].reshape(1, d)
        jobs = [whole(ab_w_out if layer % 2 == 0 else c_w_out, j),
                whole(ffn_w_in, layer), whole(ffn_w_out, layer)]
        if layer + 1 < depth:
            jobs.append(in_proj_job(layer + 1))
        if layer % 2 == 0:
            wqkv, wf, wconv = in_w
            qkv, f_rows, b_out = _ab_in(x2d, g_mix, wqkv, wf, wconv, conv_w[j], seq=seq)
            c = _fox_cumsum(f_rows, fox_f_bias[j].reshape(FOX_HEADS, 1), seq=seq)
            tq = FOX_Q_TILE
            c_row = c.reshape(FOX_HEADS, bsz, seq // tq, tq)
            a_out, cast = _fox_attn(qkv.reshape(bsz, seq, q_end), c_row, jobs, tq=tq)
            parts = [a_out.reshape(t, FOX_WIDTH), b_out]
        else:
            proj, f_logit = _c_in(x2d, g_mix, in_w[0])
            o, cast = _hgrn(proj.reshape(bsz, seq, 3 * d), f_logit.reshape(bsz, seq, d),
                            c_lower_bounds, c_head_norm[j].reshape(1, d), jobs, layer_idx=j)
            parts = [o.reshape(t, d)]
        (wo,), (w1,), (w2,) = cast[:3]
        in_w = cast[3] if layer + 1 < depth else None
        x2d = _post(parts, x2d, wo, norm_ffn[layer].reshape(1, d), w1, w2, gf,
                    final=(layer == depth - 1))
    return x2d.reshape(bsz, seq, d)
```

```python
import functools

import jax
import jax.numpy as jnp
import numpy as np
from jax import lax
from jax.experimental import pallas as pl
from jax.experimental.pallas import tpu as pltpu

D_MODEL = 1024
EPS = 1e-6
NEG_BIG = -1e30

FOX_HEADS = 8
FOX_HEAD_DIM = 64
FOX_WIDTH = FOX_HEADS * FOX_HEAD_DIM
CONV_WIDTH = D_MODEL - FOX_WIDTH
CONV_TAPS = 3
LOG2E = 1.4426950408889634
FOX_Q_SCALE = LOG2E * FOX_HEAD_DIM ** -0.5
FOX_Q_TILE = 256
FFN_COL_BLOCK = 256
HGRN_HEADS = 8
HGRN_HEAD_DIM = D_MODEL // HGRN_HEADS
HGRN_CHUNK = 64

LANES = 128
SUBLANES = 8
VMEM_LIMIT = 56 * 1024 * 1024

BF16 = jnp.bfloat16
F32 = jnp.float32

NT_DIMS = (((1,), (1,)), ((), ()))
TN_DIMS = (((0,), (0,)), ((), ()))


def _resident(shape):
    zeros = (0,) * len(shape)
    return pl.BlockSpec(shape, lambda *_: zeros, pipeline_mode=pl.Buffered(1))


def _params(*sem):
    return pltpu.CompilerParams(dimension_semantics=sem, vmem_limit_bytes=VMEM_LIMIT)


def _rms_norm(x, g):
    return x * lax.rsqrt(jnp.mean(x * x, axis=-1, keepdims=True) + EPS) * g


def _silu(x):
    half = 0.5 * x
    return half * (jnp.tanh(half) + 1.0)


BF16_SUBLANES = 2 * SUBLANES


def _cast_slabs_per_job(rows, n_steps):
    return max(k for k in range(1, n_steps + 1)
               if rows % k == 0 and (rows // k) % BF16_SUBLANES == 0)


def _cast_operands(jobs, n_steps, step_of):
    ins, in_specs, out_specs, out_shapes = [], [], [], []
    for w, layer, col_ranges in jobs:
        layers, rows, cols = w.shape
        k = _cast_slabs_per_job(rows, n_steps)
        slab = rows // k
        in_map = lambda *g, k=k, layer=layer: (layer * k + jnp.minimum(step_of(*g), k - 1), 0, 0)
        out_map = lambda *g, k=k: (jnp.minimum(step_of(*g), k - 1), 0, 0)
        ins.append(w.reshape(layers * k, slab, cols))
        in_specs.append(pl.BlockSpec((1, slab, cols), in_map))
        for c0, c1 in col_ranges:
            out_specs.append(pl.BlockSpec((1, slab, c1 - c0), out_map))
            out_shapes.append(jax.ShapeDtypeStruct((k, slab, c1 - c0), BF16))
    return ins, in_specs, out_specs, out_shapes


def _cast_slabs(jobs_cols, in_refs, out_refs):
    outs = iter(out_refs)
    for ref, col_ranges in zip(in_refs, jobs_cols):
        for c0, c1 in col_ranges:
            next(outs)[0] = ref[0, :, c0:c1].astype(BF16)


def _cast_results(jobs, outs):
    outs = iter(outs)
    return [[next(outs).reshape(w.shape[1], c1 - c0) for c0, c1 in col_ranges]
            for w, _, col_ranges in jobs]


def _ab_in_kernel(x_ref, g_ref, wqkv_ref, wf_ref, wconv_ref, cw_ref,
                  qkv_ref, f_ref, b_ref, zbuf, *, tm, seq):
    i = pl.program_id(0)

    @pl.when((i * tm) % seq == 0)
    def _():
        zbuf[0:SUBLANES, :] = jnp.zeros((SUBLANES, CONV_WIDTH), F32)

    h = _rms_norm(x_ref[...], g_ref[...]).astype(BF16)
    u = jnp.dot(h, wconv_ref[...], preferred_element_type=F32)
    u_b = u[:, :CONV_WIDTH]
    z = u[:, CONV_WIDTH:2 * CONV_WIDTH] * u[:, 2 * CONV_WIDTH:]
    zbuf[SUBLANES:SUBLANES + tm, :] = z
    cw = cw_ref[...]
    y = cw[CONV_TAPS - 1:CONV_TAPS, :] * z
    for tap in range(CONV_TAPS - 1):
        back = CONV_TAPS - 1 - tap
        y = y + cw[tap:tap + 1, :] * zbuf[SUBLANES - back:SUBLANES - back + tm, :]
    b_ref[...] = (u_b * y).astype(BF16)
    zbuf[0:SUBLANES, :] = zbuf[tm:tm + SUBLANES, :]

    f = jnp.dot(h, wf_ref[...], preferred_element_type=F32)
    f_ref[...] = jnp.transpose(f)[0:f_ref.shape[0], :]
    qkv = jnp.dot(h, wqkv_ref[...], preferred_element_type=F32)
    qkv_ref[:, :FOX_WIDTH] = (qkv[:, :FOX_WIDTH] * FOX_Q_SCALE).astype(BF16)
    qkv_ref[:, FOX_WIDTH:] = qkv[:, FOX_WIDTH:].astype(BF16)


def _ab_in(x2d, g, wqkv, wf, wconv, cw, *, seq, tm=1024):
    t = x2d.shape[0]
    kern = functools.partial(_ab_in_kernel, tm=tm, seq=seq)
    return pl.pallas_call(
        kern,
        grid=(t // tm,),
        in_specs=[
            pl.BlockSpec((tm, D_MODEL), lambda i: (i, 0)),
            _resident((1, D_MODEL)),
            _resident(wqkv.shape),
            _resident(wf.shape),
            _resident(wconv.shape),
            _resident(cw.shape),
        ],
        out_specs=[
            pl.BlockSpec((tm, 3 * FOX_WIDTH), lambda i: (i, 0)),
            pl.BlockSpec((FOX_HEADS, tm), lambda i: (0, i)),
            pl.BlockSpec((tm, CONV_WIDTH), lambda i: (i, 0)),
        ],
        out_shape=[
            jax.ShapeDtypeStruct((t, 3 * FOX_WIDTH), BF16),
            jax.ShapeDtypeStruct((FOX_HEADS, t), F32),
            jax.ShapeDtypeStruct((t, CONV_WIDTH), BF16),
        ],
        scratch_shapes=[pltpu.VMEM((tm + SUBLANES, CONV_WIDTH), F32)],
        compiler_params=_params("arbitrary"),
        name="ab_in",
    )(x2d, g, wqkv, wf, wconv, cw)


def _fox_cumsum_kernel(f_ref, bias_ref, c_ref, *, seq):
    heads, total = c_ref.shape
    x = f_ref[...] + bias_ref[...]
    ls = jnp.minimum(x, 0.0) - jnp.log(1.0 + jnp.exp(-jnp.abs(x)))
    lane = lax.broadcasted_iota(jnp.int32, (heads, LANES), 1)
    blocks = [ls[:, j * LANES:(j + 1) * LANES] for j in range(total // LANES)]
    shift = 1
    while shift < LANES:
        blocks = [blk + jnp.where(lane >= shift, pltpu.roll(blk, shift, 1), 0.0) for blk in blocks]
        shift *= 2
    per_seq = seq // LANES
    for j, blk in enumerate(blocks):
        if j % per_seq:
            blk = blk + carry
        c_ref[:, j * LANES:(j + 1) * LANES] = blk * LOG2E
        carry = blk[:, LANES - 1:LANES]


def _fox_cumsum(f_rows, bias_col, *, seq):
    heads, total = bias_col.shape[0], f_rows.shape[1]
    return pl.pallas_call(
        functools.partial(_fox_cumsum_kernel, seq=seq),
        grid=(1,),
        in_specs=[pl.BlockSpec((heads, total), lambda i: (0, 0)), _resident(bias_col.shape)],
        out_specs=pl.BlockSpec((heads, total), lambda i: (0, 0)),
        out_shape=jax.ShapeDtypeStruct((heads, total), F32),
        compiler_params=_params("arbitrary"),
        name="fox_cumsum",
    )(f_rows, bias_col)


def _fox_attn_kernel(q_ref, k_ref, v_ref, crow_ref, *rest, tq, cast_cols):
    cast_in, o_ref, cast_out = rest[:len(cast_cols)], rest[len(cast_cols)], rest[len(cast_cols) + 1:]
    _cast_slabs(cast_cols, cast_in, cast_out)
    nblk = k_ref.shape[1] // tq
    lane = lax.broadcasted_iota(jnp.int32, (1, LANES), 1)
    row = lax.broadcasted_iota(jnp.int32, (tq, tq), 0)
    col = lax.broadcasted_iota(jnp.int32, (tq, tq), 1)
    causal = row >= col
    in_head = [(lane >= hh * FOX_HEAD_DIM) & (lane < (hh + 1) * FOX_HEAD_DIM) for hh in range(2)]
    rows = lambda ref, c: ref[0, c * tq:(c + 1) * tq, :]

    for first in range(nblk // 2):
        probs = [(c, hh) for c in (first, nblk - 1 - first) for hh in range(2)]
        s = []
        for c, hh in probs:
            q = rows(q_ref, c)
            qh = jnp.where(in_head[hh], q, jnp.zeros_like(q))
            sj = []
            for j in range(c + 1):
                sh = lax.dot_general(qh, rows(k_ref, j), NT_DIMS, preferred_element_type=F32)
                sh = sh - crow_ref[hh, 0, j:j + 1, :]
                sj.append(jnp.where(causal, sh, NEG_BIG) if j == c else sh)
            s.append(sj)
        shift = []
        for (c, hh), sj in zip(probs, s):
            mx = None
            for blk in sj:
                for cb in range(tq // LANES):
                    part = blk[:, cb * LANES:(cb + 1) * LANES]
                    mx = part if mx is None else jnp.maximum(mx, part)
            ct = jnp.transpose(jnp.broadcast_to(crow_ref[hh, 0, c:c + 1, :], (SUBLANES, tq)))[:, 0:1]
            m = jnp.max(mx, axis=-1, keepdims=True) + ct
            shift.append(m - ct)
        outs = []
        for (c, hh), sj, sh in zip(probs, s, shift):
            acc = None
            for j in range(c + 1):
                v = rows(v_ref, j)
                v_aug = jnp.concatenate([v, jnp.ones_like(v)], axis=1)
                d = jnp.dot(jnp.exp2(sj[j] - sh).astype(BF16), v_aug, preferred_element_type=F32)
                acc = d if acc is None else acc + d
            outs.append(acc[:, :LANES] / acc[:, LANES:])
        for n, c in enumerate((first, nblk - 1 - first)):
            o_ref[0, c * tq:(c + 1) * tq, :] = jnp.where(
                lane < FOX_HEAD_DIM, outs[2 * n], outs[2 * n + 1]).astype(o_ref.dtype)


def _fox_attn(qkv, c_row, cast_jobs, *, tq):
    bsz, seq, _ = qkv.shape
    n_pairs = FOX_WIDTH // LANES
    nblk = seq // tq
    seq_block = lambda col0: pl.BlockSpec((1, seq, LANES), lambda b, p: (b, 0, col0 + p))
    c_ins, c_in_specs, c_out_specs, c_out_shapes = _cast_operands(
        cast_jobs, bsz * n_pairs, lambda b, p: b * n_pairs + p)
    outs = pl.pallas_call(
        functools.partial(_fox_attn_kernel, tq=tq, cast_cols=[job[-1] for job in cast_jobs]),
        grid=(bsz, n_pairs),
        in_specs=[
            seq_block(0),
            seq_block(n_pairs),
            seq_block(2 * n_pairs),
            pl.BlockSpec((2, 1, nblk, tq), lambda b, p: (p, b, 0, 0)),
        ] + c_in_specs,
        out_specs=[seq_block(0)] + c_out_specs,
        out_shape=[jax.ShapeDtypeStruct((bsz, seq, FOX_WIDTH), BF16)] + c_out_shapes,
        compiler_params=_params("parallel", "parallel"),
        name="fox_attn",
    )(qkv, qkv, qkv, c_row, *c_ins)
    return outs[0], _cast_results(cast_jobs, outs[1:])


def _post_kernel(*refs, n_parts, final):
    parts = refs[:n_parts]
    x_ref, wo_ref, g_ref, w1_ref, w2_ref, gf_ref, o_ref = refs[n_parts:]
    hidden = w2_ref.shape[0]
    mix = None
    off = 0
    for p_ref in parts:
        width = p_ref.shape[1]
        d = jnp.dot(p_ref[...], wo_ref[off:off + width, :], preferred_element_type=F32)
        mix = d if mix is None else mix + d
        off += width
    x1 = x_ref[...] + mix
    h = _rms_norm(x1, g_ref[...]).astype(BF16)
    x2 = x1
    for c0 in range(0, hidden, FFN_COL_BLOCK):
        c1 = c0 + FFN_COL_BLOCK
        gate = jnp.dot(h, w1_ref[:, c0:c1], preferred_element_type=F32)
        up = jnp.dot(h, w1_ref[:, hidden + c0:hidden + c1], preferred_element_type=F32)
        act = (_silu(gate) * up).astype(BF16)
        x2 = x2 + jnp.dot(act, w2_ref[c0:c1, :], preferred_element_type=F32)
    if final:
        x2 = _rms_norm(x2, gf_ref[...])
    o_ref[...] = x2


def _post(parts, x2d, wo, g, w1, w2, gf, *, final, tm=1024):
    t = x2d.shape[0]
    assert w2.shape[0] % FFN_COL_BLOCK == 0
    kern = functools.partial(_post_kernel, n_parts=len(parts), final=final)
    in_specs = [pl.BlockSpec((tm, p.shape[1]), lambda i: (i, 0)) for p in parts]
    in_specs += [
        pl.BlockSpec((tm, D_MODEL), lambda i: (i, 0)),
        _resident(wo.shape),
        _resident((1, D_MODEL)),
        _resident(w1.shape),
        _resident(w2.shape),
        _resident((1, D_MODEL)),
    ]
    return pl.pallas_call(
        kern,
        grid=(t // tm,),
        in_specs=in_specs,
        out_specs=pl.BlockSpec((tm, D_MODEL), lambda i: (i, 0)),
        out_shape=jax.ShapeDtypeStruct((t, D_MODEL), F32),
        compiler_params=_params("parallel"),
        name="post",
    )(*parts, x2d, wo, g, w1, w2, gf)


def _hgrn_levels():
    h, out = HGRN_CHUNK // 2, []
    while h >= 1:
        out.append(h)
        h //= 2
    return out


def _hgrn_cum_table():
    t = np.arange(HGRN_CHUNK)
    m = (t[None, :] <= t[:, None]).astype(np.float32)
    return np.concatenate([m, m, m], axis=1)


def _hgrn_masks():
    c = HGRN_CHUNK
    levels = _hgrn_levels()
    row = lax.broadcasted_iota(jnp.int32, (c, HGRN_HEAD_DIM), 0)
    r2 = lax.broadcasted_iota(jnp.int32, (c, c), 0)
    c2 = lax.broadcasted_iota(jnp.int32, (c, c), 1)
    upper = [(row & h) != 0 for h in levels]
    pair = [((r2 // (2 * h)) == (c2 // (2 * h))) & ((r2 & h) != 0) & ((c2 & h) == 0)
            for h in levels]
    sign = [jnp.where(u, 1.0, -1.0) for u in upper]
    return row, upper, pair, sign, r2 == c2


def _hgrn_block(q16, f_logit, v, g16, lb, hn, cum, st, masks):
    c = HGRN_CHUNK
    levels = _hgrn_levels()
    row, upper, pair, sign, diag = masks

    n = q16.shape[0] // c
    chunk = lambda arr, i: arr[i * c:(i + 1) * c]
    q = q16.astype(F32)
    sig = pl.reciprocal(1.0 + jnp.exp2(f_logit * (-LOG2E)), approx=True)
    f = lb + (1.0 - lb) * sig
    k = 1.0 - f
    lg = jnp.log2(f)
    hi = lg.astype(BF16)
    r1 = lg - hi.astype(F32)
    mid = r1.astype(BF16)
    lo = (r1 - mid.astype(F32)).astype(BF16)
    b = [jnp.dot(cum, jnp.concatenate([chunk(hi, i), chunk(mid, i), chunk(lo, i)], axis=0),
                 preferred_element_type=F32) for i in range(n)]

    qk = jnp.sum(q * k, axis=-1, keepdims=True)

    a = [None] * n
    o_inter = []
    for li, h in enumerate(levels):
        xs = []
        for i in range(n):
            qi, ki, bi = chunk(q, i), chunk(k, i), b[i]
            if h == 1:
                x = jnp.where(upper[li], qi * chunk(f, i), ki)
            else:
                if h >= SUBLANES:
                    ref = jnp.concatenate(
                        [jnp.broadcast_to(bi[base + h - 1:base + h, :], (2 * h, HGRN_HEAD_DIM))
                         for base in range(0, c, 2 * h)], axis=0)
                else:
                    b3 = bi.reshape(c // SUBLANES, SUBLANES, HGRN_HEAD_DIM)
                    refs = [jnp.broadcast_to(b3[:, base + h - 1:base + h, :], b3.shape).reshape(bi.shape)
                            for base in range(0, SUBLANES, 2 * h)]
                    ref = refs[0]
                    for m in range(1, len(refs)):
                        ref = jnp.where((row % SUBLANES) >= m * 2 * h, refs[m], ref)
                x = jnp.where(upper[li], qi, ki) * jnp.exp2((bi - ref) * sign[li])
            xs.append(x.astype(BF16))
        ps = [lax.dot_general(x, x, NT_DIMS, preferred_element_type=F32) for x in xs]
        a = [jnp.where(pair[li], ps[i], 0.0 if a[i] is None else a[i]) for i in range(n)]

        for i in range(li * n // len(levels), (li + 1) * n // len(levels)):
            b_last = b[i][c - 1:c, :]
            kv = lax.dot_general(chunk(v, i), (chunk(k, i) * jnp.exp2(b_last - b[i])).astype(BF16),
                                 TN_DIMS, preferred_element_type=F32)
            o_inter.append(lax.dot_general((chunk(q, i) * jnp.exp2(b[i])).astype(BF16),
                                           st.astype(BF16), NT_DIMS, preferred_element_type=F32))
            st = st * jnp.exp2(b_last) + kv
    a = [jnp.where(diag, chunk(qk, i), a[i]) for i in range(n)]
    o_intra = [jnp.dot(a[i].astype(BF16), chunk(v, i), preferred_element_type=F32) for i in range(n)]

    o = jnp.concatenate([o_inter[i] + o_intra[i] for i in range(n)], axis=0)
    o = o * lax.rsqrt(jnp.mean(o * o, axis=-1, keepdims=True) + EPS) * hn
    g = g16.astype(F32)
    return (o * _silu(g)).astype(BF16), st


def _c_in_kernel(x_ref, g_ref, w_ref, o_ref, f_ref):
    h = _rms_norm(x_ref[...], g_ref[...]).astype(BF16)
    pr = jnp.dot(h, w_ref[...], preferred_element_type=F32)
    o_ref[:, :D_MODEL] = pr[:, :D_MODEL].astype(BF16)
    o_ref[:, D_MODEL:] = pr[:, 2 * D_MODEL:].astype(BF16)
    f_ref[...] = pr[:, D_MODEL:2 * D_MODEL]


def _c_in(x2d, g, w, *, tm=1024):
    t = x2d.shape[0]
    return pl.pallas_call(
        _c_in_kernel,
        grid=(t // tm,),
        in_specs=[
            pl.BlockSpec((tm, D_MODEL), lambda i: (i, 0)),
            _resident((1, D_MODEL)),
            _resident(w.shape),
        ],
        out_specs=[
            pl.BlockSpec((tm, 3 * D_MODEL), lambda i: (i, 0)),
            pl.BlockSpec((tm, D_MODEL), lambda i: (i, 0)),
        ],
        out_shape=[
            jax.ShapeDtypeStruct((t, 3 * D_MODEL), BF16),
            jax.ShapeDtypeStruct((t, D_MODEL), F32),
        ],
        compiler_params=_params("parallel"),
        name="c_in",
    )(x2d, g, w)


HGRN_HEADS_PER_STEP = 2


def _hgrn_kernel(q_ref, f_ref, i_ref, g_ref, lbraw_ref, hn_ref, cum_ref, *rest,
                 layer_idx, cast_cols):
    n_cast = len(cast_cols)
    cast_in, o_ref, cast_out, state = rest[:n_cast], rest[n_cast], rest[n_cast + 1:-1], rest[-1]
    _cast_slabs(cast_cols, cast_in, cast_out)

    @pl.when(pl.program_id(2) == 0)
    def _():
        state[...] = jnp.zeros_like(state)

    raw = lbraw_ref[...]
    e = jnp.exp(raw - jnp.max(raw, axis=0, keepdims=True))
    sm = e / jnp.sum(e, axis=0, keepdims=True)
    lb = jnp.sum(sm[0:layer_idx + 1, :], axis=0, keepdims=True) - sm[0:1, :]
    hn = hn_ref[...]
    masks = _hgrn_masks()
    for n in range(HGRN_HEADS_PER_STEP):
        hs = slice(n * HGRN_HEAD_DIM, (n + 1) * HGRN_HEAD_DIM)
        o, st = _hgrn_block(q_ref[0, :, hs], f_ref[0, :, hs], i_ref[0, :, hs], g_ref[0, :, hs],
                            lb[:, hs], hn[:, hs], cum_ref[...], state[n], masks)
        state[n] = st
        o_ref[0, :, hs] = o


def _hgrn(proj, f_logit, lb_raw, head_norm, cast_jobs, *, layer_idx, blk=2048):
    bsz, seq, _ = f_logit.shape
    hd = HGRN_HEAD_DIM
    per = HGRN_HEADS_PER_STEP
    width = per * hd
    ng = HGRN_HEADS // per
    n_l = seq // blk
    cum = jnp.asarray(_hgrn_cum_table(), BF16)
    c_ins, c_in_specs, c_out_specs, c_out_shapes = _cast_operands(
        cast_jobs, bsz * ng * n_l, lambda b, h, l: (b * ng + h) * n_l + l)
    outs = pl.pallas_call(
        functools.partial(_hgrn_kernel, layer_idx=layer_idx,
                          cast_cols=[job[-1] for job in cast_jobs]),
        grid=(bsz, ng, n_l),
        in_specs=[
            pl.BlockSpec((1, blk, width), lambda b, h, l: (b, l, h)),
            pl.BlockSpec((1, blk, width), lambda b, h, l: (b, l, h)),
            pl.BlockSpec((1, blk, width), lambda b, h, l: (b, l, ng + h)),
            pl.BlockSpec((1, blk, width), lambda b, h, l: (b, l, 2 * ng + h)),
            pl.BlockSpec((lb_raw.shape[0], width), lambda b, h, l: (0, h)),
            pl.BlockSpec((1, width), lambda b, h, l: (0, h)),
            _resident(cum.shape),
        ] + c_in_specs,
        out_specs=[pl.BlockSpec((1, blk, width), lambda b, h, l: (b, l, h))] + c_out_specs,
        out_shape=[jax.ShapeDtypeStruct((bsz, seq, D_MODEL), BF16)] + c_out_shapes,
        scratch_shapes=[pltpu.VMEM((per, hd, hd), F32)],
        compiler_params=_params("parallel", "parallel", "arbitrary"),
        name="hgrn",
    )(proj, f_logit, proj, proj, lb_raw, head_norm, cum, *c_ins)
    return outs[0], _cast_results(cast_jobs, outs[1:])


def kernel(x, norm_mix, norm_ffn, final_norm, ab_w_in, fox_f_bias, conv_w, ab_w_out,
           c_w_in, c_lower_bounds, c_head_norm, c_w_out, ffn_w_in, ffn_w_out):
    bsz, seq, d = x.shape
    depth = norm_mix.shape[0]
    t = bsz * seq
    x2d = x.reshape(t, d)
    gf = final_norm.reshape(1, d)
    q_end = 3 * FOX_WIDTH
    f_end = q_end + FOX_HEADS

    ab_cols = [(0, q_end), (q_end, q_end + LANES), (f_end, f_end + 3 * CONV_WIDTH)]
    whole = lambda w, layer: (w, layer, [(0, w.shape[2])])

    def in_proj_job(layer):
        j = layer // 2
        return (ab_w_in, j, ab_cols) if layer % 2 == 0 else whole(c_w_in, j)

    w0, j0, cols0 = in_proj_job(0)
    in_w = [w0[j0, :, c0:c1].astype(BF16) for c0, c1 in cols0]

    for layer in range(depth):
        j = layer // 2
        g_mix = norm_mix[layer].reshape(1, d)
        jobs = [whole(ab_w_out if layer % 2 == 0 else c_w_out, j),
                whole(ffn_w_in, layer), whole(ffn_w_out, layer)]
        if layer + 1 < depth:
            jobs.append(in_proj_job(layer + 1))
        if layer % 2 == 0:
            wqkv, wf, wconv = in_w
            qkv, f_rows, b_out = _ab_in(x2d, g_mix, wqkv, wf, wconv, conv_w[j], seq=seq)
            c = _fox_cumsum(f_rows, fox_f_bias[j].reshape(FOX_HEADS, 1), seq=seq)
            tq = FOX_Q_TILE
            c_row = c.reshape(FOX_HEADS, bsz, seq // tq, tq)
            a_out, cast = _fox_attn(qkv.reshape(bsz, seq, q_end), c_row, jobs, tq=tq)
            parts = [a_out.reshape(t, FOX_WIDTH), b_out]
        else:
            proj, f_logit = _c_in(x2d, g_mix, in_w[0])
            o, cast = _hgrn(proj.reshape(bsz, seq, 3 * d), f_logit.reshape(bsz, seq, d),
                            c_lower_bounds, c_head_norm[j].reshape(1, d), jobs, layer_idx=j)
            parts = [o.reshape(t, d)]
        (wo,), (w1,), (w2,) = cast[:3]
        in_w = cast[3] if layer + 1 < depth else None
        x2d = _post(parts, x2d, wo, norm_ffn[layer].reshape(1, d), w1, w2, gf,
                    final=(layer == depth - 1))
    return x2d.reshape(bsz, seq, d)
```

```python
import functools

import jax
import jax.numpy as jnp
import numpy as np
from jax import lax
from jax.experimental import pallas as pl
from jax.experimental.pallas import tpu as pltpu

D_MODEL = 1024
EPS = 1e-6
NEG_BIG = -1e30

FOX_HEADS = 8
FOX_HEAD_DIM = 64
FOX_WIDTH = FOX_HEADS * FOX_HEAD_DIM
CONV_WIDTH = D_MODEL - FOX_WIDTH
CONV_TAPS = 3
LOG2E = 1.4426950408889634
FOX_Q_SCALE = LOG2E * FOX_HEAD_DIM ** -0.5
FOX_Q_TILE = 256
FFN_COL_BLOCK = 256
HGRN_HEADS = 8
HGRN_HEAD_DIM = D_MODEL // HGRN_HEADS
HGRN_CHUNK = 64

LANES = 128
SUBLANES = 8
VMEM_LIMIT = 56 * 1024 * 1024

BF16 = jnp.bfloat16
F32 = jnp.float32

NT_DIMS = (((1,), (1,)), ((), ()))
TN_DIMS = (((0,), (0,)), ((), ()))


def _resident(shape):
    zeros = (0,) * len(shape)
    return pl.BlockSpec(shape, lambda *_: zeros, pipeline_mode=pl.Buffered(1))


def _params(*sem):
    return pltpu.CompilerParams(dimension_semantics=sem, vmem_limit_bytes=VMEM_LIMIT)


def _rms_norm(x, g):
    return x * lax.rsqrt(jnp.mean(x * x, axis=-1, keepdims=True) + EPS) * g


def _silu(x):
    half = 0.5 * x
    return half * (jnp.tanh(half) + 1.0)


BF16_SUBLANES = 2 * SUBLANES


def _cast_slabs_per_job(rows, n_steps):
    return max(k for k in range(1, n_steps + 1)
               if rows % k == 0 and (rows // k) % BF16_SUBLANES == 0)


def _cast_operands(jobs, n_steps, step_of):
    ins, in_specs, out_specs, out_shapes = [], [], [], []
    for w, layer, col_ranges in jobs:
        layers, rows, cols = w.shape
        k = _cast_slabs_per_job(rows, n_steps)
        slab = rows // k
        in_map = lambda *g, k=k, layer=layer: (layer * k + jnp.minimum(step_of(*g), k - 1), 0, 0)
        out_map = lambda *g, k=k: (jnp.minimum(step_of(*g), k - 1), 0, 0)
        ins.append(w.reshape(layers * k, slab, cols))
        in_specs.append(pl.BlockSpec((1, slab, cols), in_map))
        for c0, c1 in col_ranges:
            out_specs.append(pl.BlockSpec((1, slab, c1 - c0), out_map))
            out_shapes.append(jax.ShapeDtypeStruct((k, slab, c1 - c0), BF16))
    return ins, in_specs, out_specs, out_shapes


def _cast_slabs(jobs_cols, in_refs, out_refs):
    outs = iter(out_refs)
    for ref, col_ranges in zip(in_refs, jobs_cols):
        for c0, c1 in col_ranges:
            next(outs)[0] = ref[0, :, c0:c1].astype(BF16)


def _cast_results(jobs, outs):
    outs = iter(outs)
    return [[next(outs).reshape(w.shape[1], c1 - c0) for c0, c1 in col_ranges]
            for w, _, col_ranges in jobs]


def _ab_in_kernel(x_ref, g_ref, wqkv_ref, wf_ref, wconv_ref, cw_ref,
                  qkv_ref, f_ref, b_ref, zbuf, *, tm, seq):
    i = pl.program_id(0)

    @pl.when((i * tm) % seq == 0)
    def _():
        zbuf[0:SUBLANES, :] = jnp.zeros((SUBLANES, CONV_WIDTH), F32)

    h = _rms_norm(x_ref[...], g_ref[...]).astype(BF16)
    u = jnp.dot(h, wconv_ref[...], preferred_element_type=F32)
    u_b = u[:, :CONV_WIDTH]
    z = u[:, CONV_WIDTH:2 * CONV_WIDTH] * u[:, 2 * CONV_WIDTH:]
    zbuf[SUBLANES:SUBLANES + tm, :] = z
    cw = cw_ref[...]
    y = cw[CONV_TAPS - 1:CONV_TAPS, :] * z
    for tap in range(CONV_TAPS - 1):
        back = CONV_TAPS - 1 - tap
        y = y + cw[tap:tap + 1, :] * zbuf[SUBLANES - back:SUBLANES - back + tm, :]
    b_ref[...] = (u_b * y).astype(BF16)
    zbuf[0:SUBLANES, :] = zbuf[tm:tm + SUBLANES, :]

    f = jnp.dot(h, wf_ref[...], preferred_element_type=F32)
    f_ref[...] = jnp.transpose(f)[0:f_ref.shape[0], :]
    qkv = jnp.dot(h, wqkv_ref[...], preferred_element_type=F32)
    qkv_ref[:, :FOX_WIDTH] = (qkv[:, :FOX_WIDTH] * FOX_Q_SCALE).astype(BF16)
    qkv_ref[:, FOX_WIDTH:] = qkv[:, FOX_WIDTH:].astype(BF16)


def _ab_in(x2d, g, wqkv, wf, wconv, cw, *, seq, tm=1024):
    t = x2d.shape[0]
    kern = functools.partial(_ab_in_kernel, tm=tm, seq=seq)
    return pl.pallas_call(
        kern,
        grid=(t // tm,),
        in_specs=[
            pl.BlockSpec((tm, D_MODEL), lambda i: (i, 0)),
            _resident((1, D_MODEL)),
            _resident(wqkv.shape),
            _resident(wf.shape),
            _resident(wconv.shape),
            _resident(cw.shape),
        ],
        out_specs=[
            pl.BlockSpec((tm, 3 * FOX_WIDTH), lambda i: (i, 0)),
            pl.BlockSpec((FOX_HEADS, tm), lambda i: (0, i)),
            pl.BlockSpec((tm, CONV_WIDTH), lambda i: (i, 0)),
        ],
        out_shape=[
            jax.ShapeDtypeStruct((t, 3 * FOX_WIDTH), BF16),
            jax.ShapeDtypeStruct((FOX_HEADS, t), F32),
            jax.ShapeDtypeStruct((t, CONV_WIDTH), BF16),
        ],
        scratch_shapes=[pltpu.VMEM((tm + SUBLANES, CONV_WIDTH), F32)],
        compiler_params=_params("arbitrary"),
        name="ab_in",
    )(x2d, g, wqkv, wf, wconv, cw)


def _fox_cumsum_kernel(f_ref, bias_ref, c_ref, *, seq):
    heads, total = c_ref.shape
    x = f_ref[...] + bias_ref[...]
    ls = jnp.minimum(x, 0.0) - jnp.log(1.0 + jnp.exp(-jnp.abs(x)))
    lane = lax.broadcasted_iota(jnp.int32, (heads, LANES), 1)
    blocks = [ls[:, j * LANES:(j + 1) * LANES] for j in range(total // LANES)]
    shift = 1
    while shift < LANES:
        blocks = [blk + jnp.where(lane >= shift, pltpu.roll(blk, shift, 1), 0.0) for blk in blocks]
        shift *= 2
    per_seq = seq // LANES
    for j, blk in enumerate(blocks):
        if j % per_seq:
            blk = blk + carry
        c_ref[:, j * LANES:(j + 1) * LANES] = blk * LOG2E
        carry = blk[:, LANES - 1:LANES]


def _fox_cumsum(f_rows, bias_col, *, seq):
    heads, total = bias_col.shape[0], f_rows.shape[1]
    return pl.pallas_call(
        functools.partial(_fox_cumsum_kernel, seq=seq),
        grid=(1,),
        in_specs=[pl.BlockSpec((heads, total), lambda i: (0, 0)), _resident(bias_col.shape)],
        out_specs=pl.BlockSpec((heads, total), lambda i: (0, 0)),
        out_shape=jax.ShapeDtypeStruct((heads, total), F32),
        compiler_params=_params("arbitrary"),
        name="fox_cumsum",
    )(f_rows, bias_col)


def _fox_attn_kernel(q_ref, k_ref, v_ref, crow_ref, *rest, tq, cast_cols):
    cast_in, o_ref, cast_out = rest[:len(cast_cols)], rest[len(cast_cols)], rest[len(cast_cols) + 1:]
    _cast_slabs(cast_cols, cast_in, cast_out)
    nblk = k_ref.shape[1] // tq
    lane = lax.broadcasted_iota(jnp.int32, (1, LANES), 1)
    row = lax.broadcasted_iota(jnp.int32, (tq, tq), 0)
    col = lax.broadcasted_iota(jnp.int32, (tq, tq), 1)
    causal = row >= col
    in_head = [(lane >= hh * FOX_HEAD_DIM) & (lane < (hh + 1) * FOX_HEAD_DIM) for hh in range(2)]
    rows = lambda ref, c: ref[0, c * tq:(c + 1) * tq, :]

    probs = [(c, hh) for c in range(nblk) for hh in range(2)]
    s = []
    for c, hh in probs:
        q = rows(q_ref, c)
        qh = jnp.where(in_head[hh], q, jnp.zeros_like(q))
        sj = []
        for j in range(c + 1):
            sh = lax.dot_general(qh, rows(k_ref, j), NT_DIMS, preferred_element_type=F32)
            sh = sh - crow_ref[hh, 0, j:j + 1, :]
            sj.append(jnp.where(causal, sh, NEG_BIG) if j == c else sh)
        s.append(sj)
    shift = []
    for (c, hh), sj in zip(probs, s):
        mx = None
        for blk in sj:
            for cb in range(tq // LANES):
                part = blk[:, cb * LANES:(cb + 1) * LANES]
                mx = part if mx is None else jnp.maximum(mx, part)
        ct = jnp.transpose(jnp.broadcast_to(crow_ref[hh, 0, c:c + 1, :], (SUBLANES, tq)))[:, 0:1]
        m = jnp.max(mx, axis=-1, keepdims=True) + ct
        shift.append(m - ct)
    outs = []
    for (c, hh), sj, sh in zip(probs, s, shift):
        acc = None
        for j in range(c + 1):
            v = rows(v_ref, j)
            v_aug = jnp.concatenate([v, jnp.ones_like(v)], axis=1)
            d = jnp.dot(jnp.exp2(sj[j] - sh).astype(BF16), v_aug, preferred_element_type=F32)
            acc = d if acc is None else acc + d
        outs.append(acc[:, :LANES] / acc[:, LANES:])
    for c in range(nblk):
        o_ref[0, c * tq:(c + 1) * tq, :] = jnp.where(
            lane < FOX_HEAD_DIM, outs[2 * c], outs[2 * c + 1]).astype(o_ref.dtype)


def _fox_attn(qkv, c_row, cast_jobs, *, tq):
    bsz, seq, _ = qkv.shape
    n_pairs = FOX_WIDTH // LANES
    nblk = seq // tq
    seq_block = lambda col0: pl.BlockSpec((1, seq, LANES), lambda b, p: (b, 0, col0 + p))
    c_ins, c_in_specs, c_out_specs, c_out_shapes = _cast_operands(
        cast_jobs, bsz * n_pairs, lambda b, p: b * n_pairs + p)
    outs = pl.pallas_call(
        functools.partial(_fox_attn_kernel, tq=tq, cast_cols=[job[-1] for job in cast_jobs]),
        grid=(bsz, n_pairs),
        in_specs=[
            seq_block(0),
            seq_block(n_pairs),
            seq_block(2 * n_pairs),
            pl.BlockSpec((2, 1, nblk, tq), lambda b, p: (p, b, 0, 0)),
        ] + c_in_specs,
        out_specs=[seq_block(0)] + c_out_specs,
        out_shape=[jax.ShapeDtypeStruct((bsz, seq, FOX_WIDTH), BF16)] + c_out_shapes,
        compiler_params=_params("parallel", "parallel"),
        name="fox_attn",
    )(qkv, qkv, qkv, c_row, *c_ins)
    return outs[0], _cast_results(cast_jobs, outs[1:])


def _post_kernel(*refs, n_parts, final):
    parts = refs[:n_parts]
    x_ref, wo_ref, g_ref, w1_ref, w2_ref, gf_ref, o_ref = refs[n_parts:]
    hidden = w2_ref.shape[0]
    mix = None
    off = 0
    for p_ref in parts:
        width = p_ref.shape[1]
        d = jnp.dot(p_ref[...], wo_ref[off:off + width, :], preferred_element_type=F32)
        mix = d if mix is None else mix + d
        off += width
    x1 = x_ref[...] + mix
    h = _rms_norm(x1, g_ref[...]).astype(BF16)
    x2 = x1
    for c0 in range(0, hidden, FFN_COL_BLOCK):
        c1 = c0 + FFN_COL_BLOCK
        gate = jnp.dot(h, w1_ref[:, c0:c1], preferred_element_type=F32)
        up = jnp.dot(h, w1_ref[:, hidden + c0:hidden + c1], preferred_element_type=F32)
        act = (_silu(gate) * up).astype(BF16)
        x2 = x2 + jnp.dot(act, w2_ref[c0:c1, :], preferred_element_type=F32)
    if final:
        x2 = _rms_norm(x2, gf_ref[...])
    o_ref[...] = x2


def _post(parts, x2d, wo, g, w1, w2, gf, *, final, tm=1024):
    t = x2d.shape[0]
    assert w2.shape[0] % FFN_COL_BLOCK == 0
    kern = functools.partial(_post_kernel, n_parts=len(parts), final=final)
    in_specs = [pl.BlockSpec((tm, p.shape[1]), lambda i: (i, 0)) for p in parts]
    in_specs += [
        pl.BlockSpec((tm, D_MODEL), lambda i: (i, 0)),
        _resident(wo.shape),
        _resident((1, D_MODEL)),
        _resident(w1.shape),
        _resident(w2.shape),
        _resident((1, D_MODEL)),
    ]
    return pl.pallas_call(
        kern,
        grid=(t // tm,),
        in_specs=in_specs,
        out_specs=pl.BlockSpec((tm, D_MODEL), lambda i: (i, 0)),
        out_shape=jax.ShapeDtypeStruct((t, D_MODEL), F32),
        compiler_params=_params("parallel"),
        name="post",
    )(*parts, x2d, wo, g, w1, w2, gf)


def _hgrn_levels():
    h, out = HGRN_CHUNK // 2, []
    while h >= 1:
        out.append(h)
        h //= 2
    return out


def _hgrn_cum_table():
    t = np.arange(HGRN_CHUNK)
    m = (t[None, :] <= t[:, None]).astype(np.float32)
    return np.concatenate([m, m, m], axis=1)


def _hgrn_masks():
    c = HGRN_CHUNK
    levels = _hgrn_levels()
    row = lax.broadcasted_iota(jnp.int32, (c, HGRN_HEAD_DIM), 0)
    r2 = lax.broadcasted_iota(jnp.int32, (c, c), 0)
    c2 = lax.broadcasted_iota(jnp.int32, (c, c), 1)
    upper = [(row & h) != 0 for h in levels]
    pair = [((r2 // (2 * h)) == (c2 // (2 * h))) & ((r2 & h) != 0) & ((c2 & h) == 0)
            for h in levels]
    sign = [jnp.where(u, 1.0, -1.0) for u in upper]
    return row, upper, pair, sign, r2 == c2


def _hgrn_block(q16, f_logit, v, g16, lb, hn, cum, st, masks):
    c = HGRN_CHUNK
    levels = _hgrn_levels()
    row, upper, pair, sign, diag = masks

    n = q16.shape[0] // c
    chunk = lambda arr, i: arr[i * c:(i + 1) * c]
    q = q16.astype(F32)
    sig = pl.reciprocal(1.0 + jnp.exp2(f_logit * (-LOG2E)), approx=True)
    f = lb + (1.0 - lb) * sig
    k = 1.0 - f
    lg = jnp.log2(f)
    hi = lg.astype(BF16)
    r1 = lg - hi.astype(F32)
    mid = r1.astype(BF16)
    lo = (r1 - mid.astype(F32)).astype(BF16)
    b = [jnp.dot(cum, jnp.concatenate([chunk(hi, i), chunk(mid, i), chunk(lo, i)], axis=0),
                 preferred_element_type=F32) for i in range(n)]

    qk = jnp.sum(q * k, axis=-1, keepdims=True)

    a = [None] * n
    o_inter = []
    for li, h in enumerate(levels):
        xs = []
        for i in range(n):
            qi, ki, bi = chunk(q, i), chunk(k, i), b[i]
            if h == 1:
                x = jnp.where(upper[li], qi * chunk(f, i), ki)
            else:
                if h >= SUBLANES:
                    ref = jnp.concatenate(
                        [jnp.broadcast_to(bi[base + h - 1:base + h, :], (2 * h, HGRN_HEAD_DIM))
                         for base in range(0, c, 2 * h)], axis=0)
                else:
                    b3 = bi.reshape(c // SUBLANES, SUBLANES, HGRN_HEAD_DIM)
                    refs = [jnp.broadcast_to(b3[:, base + h - 1:base + h, :], b3.shape).reshape(bi.shape)
                            for base in range(0, SUBLANES, 2 * h)]
                    ref = refs[0]
                    for m in range(1, len(refs)):
                        ref = jnp.where((row % SUBLANES) >= m * 2 * h, refs[m], ref)
                x = jnp.where(upper[li], qi, ki) * jnp.exp2((bi - ref) * sign[li])
            xs.append(x.astype(BF16))
        ps = [lax.dot_general(x, x, NT_DIMS, preferred_element_type=F32) for x in xs]
        a = [jnp.where(pair[li], ps[i], 0.0 if a[i] is None else a[i]) for i in range(n)]

        for i in range(li * n // len(levels), (li + 1) * n // len(levels)):
            b_last = b[i][c - 1:c, :]
            kv = lax.dot_general(chunk(v, i), (chunk(k, i) * jnp.exp2(b_last - b[i])).astype(BF16),
                                 TN_DIMS, preferred_element_type=F32)
            o_inter.append(lax.dot_general((chunk(q, i) * jnp.exp2(b[i])).astype(BF16),
                                           st.astype(BF16), NT_DIMS, preferred_element_type=F32))
            st = st * jnp.exp2(b_last) + kv
    a = [jnp.where(diag, chunk(qk, i), a[i]) for i in range(n)]
    o_intra = [jnp.dot(a[i].astype(BF16), chunk(v, i), preferred_element_type=F32) for i in range(n)]

    o = jnp.concatenate([o_inter[i] + o_intra[i] for i in range(n)], axis=0)
    o = o * lax.rsqrt(jnp.mean(o * o, axis=-1, keepdims=True) + EPS) * hn
    g = g16.astype(F32)
    return (o * _silu(g)).astype(BF16), st


def _c_in_kernel(x_ref, g_ref, w_ref, o_ref, f_ref):
    h = _rms_norm(x_ref[...], g_ref[...]).astype(BF16)
    pr = jnp.dot(h, w_ref[...], preferred_element_type=F32)
    o_ref[:, :D_MODEL] = pr[:, :D_MODEL].astype(BF16)
    o_ref[:, D_MODEL:] = pr[:, 2 * D_MODEL:].astype(BF16)
    f_ref[...] = pr[:, D_MODEL:2 * D_MODEL]


def _c_in(x2d, g, w, *, tm=1024):
    t = x2d.shape[0]
    return pl.pallas_call(
        _c_in_kernel,
        grid=(t // tm,),
        in_specs=[
            pl.BlockSpec((tm, D_MODEL), lambda i: (i, 0)),
            _resident((1, D_MODEL)),
            _resident(w.shape),
        ],
        out_specs=[
            pl.BlockSpec((tm, 3 * D_MODEL), lambda i: (i, 0)),
            pl.BlockSpec((tm, D_MODEL), lambda i: (i, 0)),
        ],
        out_shape=[
            jax.ShapeDtypeStruct((t, 3 * D_MODEL), BF16),
            jax.ShapeDtypeStruct((t, D_MODEL), F32),
        ],
        compiler_params=_params("parallel"),
        name="c_in",
    )(x2d, g, w)


HGRN_HEADS_PER_STEP = 2


def _hgrn_kernel(q_ref, f_ref, i_ref, g_ref, lbraw_ref, hn_ref, cum_ref, *rest,
                 layer_idx, cast_cols):
    n_cast = len(cast_cols)
    cast_in, o_ref, cast_out, state = rest[:n_cast], rest[n_cast], rest[n_cast + 1:-1], rest[-1]
    _cast_slabs(cast_cols, cast_in, cast_out)

    @pl.when(pl.program_id(2) == 0)
    def _():
        state[...] = jnp.zeros_like(state)

    raw = lbraw_ref[...]
    e = jnp.exp(raw - jnp.max(raw, axis=0, keepdims=True))
    sm = e / jnp.sum(e, axis=0, keepdims=True)
    lb = jnp.sum(sm[0:layer_idx + 1, :], axis=0, keepdims=True) - sm[0:1, :]
    hn = hn_ref[...]
    masks = _hgrn_masks()
    for n in range(HGRN_HEADS_PER_STEP):
        hs = slice(n * HGRN_HEAD_DIM, (n + 1) * HGRN_HEAD_DIM)
        o, st = _hgrn_block(q_ref[0, :, hs], f_ref[0, :, hs], i_ref[0, :, hs], g_ref[0, :, hs],
                            lb[:, hs], hn[:, hs], cum_ref[...], state[n], masks)
        state[n] = st
        o_ref[0, :, hs] = o


def _hgrn(proj, f_logit, lb_raw, head_norm, cast_jobs, *, layer_idx, blk=2048):
    bsz, seq, _ = f_logit.shape
    hd = HGRN_HEAD_DIM
    per = HGRN_HEADS_PER_STEP
    width = per * hd
    ng = HGRN_HEADS // per
    n_l = seq // blk
    cum = jnp.asarray(_hgrn_cum_table(), BF16)
    c_ins, c_in_specs, c_out_specs, c_out_shapes = _cast_operands(
        cast_jobs, bsz * ng * n_l, lambda b, h, l: (b * ng + h) * n_l + l)
    outs = pl.pallas_call(
        functools.partial(_hgrn_kernel, layer_idx=layer_idx,
                          cast_cols=[job[-1] for job in cast_jobs]),
        grid=(bsz, ng, n_l),
        in_specs=[
            pl.BlockSpec((1, blk, width), lambda b, h, l: (b, l, h)),
            pl.BlockSpec((1, blk, width), lambda b, h, l: (b, l, h)),
            pl.BlockSpec((1, blk, width), lambda b, h, l: (b, l, ng + h)),
            pl.BlockSpec((1, blk, width), lambda b, h, l: (b, l, 2 * ng + h)),
            pl.BlockSpec((lb_raw.shape[0], width), lambda b, h, l: (0, h)),
            pl.BlockSpec((1, width), lambda b, h, l: (0, h)),
            _resident(cum.shape),
        ] + c_in_specs,
        out_specs=[pl.BlockSpec((1, blk, width), lambda b, h, l: (b, l, h))] + c_out_specs,
        out_shape=[jax.ShapeDtypeStruct((bsz, seq, D_MODEL), BF16)] + c_out_shapes,
        scratch_shapes=[pltpu.VMEM((per, hd, hd), F32)],
        compiler_params=_params("parallel", "parallel", "arbitrary"),
        name="hgrn",
    )(proj, f_logit, proj, proj, lb_raw, head_norm, cum, *c_ins)
    return outs[0], _cast_results(cast_jobs, outs[1:])


def kernel(x, norm_mix, norm_ffn, final_norm, ab_w_in, fox_f_bias, conv_w, ab_w_out,
           c_w_in, c_lower_bounds, c_head_norm, c_w_out, ffn_w_in, ffn_w_out):
    bsz, seq, d = x.shape
    depth = norm_mix.shape[0]
    t = bsz * seq
    x2d = x.reshape(t, d)
    gf = final_norm.reshape(1, d)
    q_end = 3 * FOX_WIDTH
    f_end = q_end + FOX_HEADS

    ab_cols = [(0, q_end), (q_end, q_end + LANES), (f_end, f_end + 3 * CONV_WIDTH)]
    whole = lambda w, layer: (w, layer, [(0, w.shape[2])])

    def in_proj_job(layer):
        j = layer // 2
        return (ab_w_in, j, ab_cols) if layer % 2 == 0 else whole(c_w_in, j)

    w0, j0, cols0 = in_proj_job(0)
    in_w = [w0[j0, :, c0:c1].astype(BF16) for c0, c1 in cols0]

    for layer in range(depth):
        j = layer // 2
        g_mix = norm_mix[layer].reshape(1, d)
        jobs = [whole(ab_w_out if layer % 2 == 0 else c_w_out, j),
                whole(ffn_w_in, layer), whole(ffn_w_out, layer)]
        if layer + 1 < depth:
            jobs.append(in_proj_job(layer + 1))
        if layer % 2 == 0:
            wqkv, wf, wconv = in_w
            qkv, f_rows, b_out = _ab_in(x2d, g_mix, wqkv, wf, wconv, conv_w[j], seq=seq)
            c = _fox_cumsum(f_rows, fox_f_bias[j].reshape(FOX_HEADS, 1), seq=seq)
            tq = FOX_Q_TILE
            c_row = c.reshape(FOX_HEADS, bsz, seq // tq, tq)
            a_out, cast = _fox_attn(qkv.reshape(bsz, seq, q_end), c_row, jobs, tq=tq)
            parts = [a_out.reshape(t, FOX_WIDTH), b_out]
        else:
            proj, f_logit = _c_in(x2d, g_mix, in_w[0])
            o, cast = _hgrn(proj.reshape(bsz, seq, 3 * d), f_logit.reshape(bsz, seq, d),
                            c_lower_bounds, c_head_norm[j].reshape(1, d), jobs, layer_idx=j)
            parts = [o.reshape(t, d)]
        (wo,), (w1,), (w2,) = cast[:3]
        in_w = cast[3] if layer + 1 < depth else None
        x2d = _post(parts, x2d, wo, norm_ffn[layer].reshape(1, d), w1, w2, gf,
                    final=(layer == depth - 1))
    return x2d.reshape(bsz, seq, d)
```

```python
import functools

import jax
import jax.numpy as jnp
import numpy as np
from jax import lax
from jax.experimental import pallas as pl
from jax.experimental.pallas import tpu as pltpu

D_MODEL = 1024
EPS = 1e-6
NEG_BIG = -1e30

FOX_HEADS = 8
FOX_HEAD_DIM = 64
FOX_WIDTH = FOX_HEADS * FOX_HEAD_DIM
CONV_WIDTH = D_MODEL - FOX_WIDTH
CONV_TAPS = 3
LOG2E = 1.4426950408889634
FOX_Q_SCALE = LOG2E * FOX_HEAD_DIM ** -0.5
FOX_Q_TILE = 256
FOX_PAIRS_PER_STEP = 2
FFN_COL_BLOCK = 256
HGRN_HEADS = 8
HGRN_HEAD_DIM = D_MODEL // HGRN_HEADS
HGRN_CHUNK = 64

LANES = 128
SUBLANES = 8
VMEM_LIMIT = 56 * 1024 * 1024

BF16 = jnp.bfloat16
F32 = jnp.float32

NT_DIMS = (((1,), (1,)), ((), ()))
TN_DIMS = (((0,), (0,)), ((), ()))


def _resident(shape):
    zeros = (0,) * len(shape)
    return pl.BlockSpec(shape, lambda *_: zeros, pipeline_mode=pl.Buffered(1))


def _params(*sem):
    return pltpu.CompilerParams(dimension_semantics=sem, vmem_limit_bytes=VMEM_LIMIT)


def _rms_norm(x, g):
    return x * lax.rsqrt(jnp.mean(x * x, axis=-1, keepdims=True) + EPS) * g


def _silu(x):
    half = 0.5 * x
    return half * (jnp.tanh(half) + 1.0)


BF16_SUBLANES = 2 * SUBLANES


def _cast_slabs_per_job(rows, n_steps):
    return max(k for k in range(1, n_steps + 1)
               if rows % k == 0 and (rows // k) % BF16_SUBLANES == 0)


def _cast_operands(jobs, n_steps, step_of):
    ins, in_specs, out_specs, out_shapes = [], [], [], []
    for w, layer, col_ranges in jobs:
        layers, rows, cols = w.shape
        k = _cast_slabs_per_job(rows, n_steps)
        slab = rows // k
        in_map = lambda *g, k=k, layer=layer: (layer * k + jnp.minimum(step_of(*g), k - 1), 0, 0)
        out_map = lambda *g, k=k: (jnp.minimum(step_of(*g), k - 1), 0, 0)
        ins.append(w.reshape(layers * k, slab, cols))
        in_specs.append(pl.BlockSpec((1, slab, cols), in_map))
        for c0, c1 in col_ranges:
            out_specs.append(pl.BlockSpec((1, slab, c1 - c0), out_map))
            out_shapes.append(jax.ShapeDtypeStruct((k, slab, c1 - c0), BF16))
    return ins, in_specs, out_specs, out_shapes


def _cast_slabs(jobs_cols, in_refs, out_refs):
    outs = iter(out_refs)
    for ref, col_ranges in zip(in_refs, jobs_cols):
        for c0, c1 in col_ranges:
            next(outs)[0] = ref[0, :, c0:c1].astype(BF16)


def _cast_results(jobs, outs):
    outs = iter(outs)
    return [[next(outs).reshape(w.shape[1], c1 - c0) for c0, c1 in col_ranges]
            for w, _, col_ranges in jobs]


def _ab_in_kernel(x_ref, g_ref, wqkv_ref, wf_ref, wconv_ref, cw_ref,
                  qkv_ref, f_ref, b_ref, zbuf, *, tm, seq):
    i = pl.program_id(0)

    @pl.when((i * tm) % seq == 0)
    def _():
        zbuf[0:SUBLANES, :] = jnp.zeros((SUBLANES, CONV_WIDTH), F32)

    h = _rms_norm(x_ref[...], g_ref[...]).astype(BF16)
    u = jnp.dot(h, wconv_ref[...], preferred_element_type=F32)
    u_b = u[:, :CONV_WIDTH]
    z = u[:, CONV_WIDTH:2 * CONV_WIDTH] * u[:, 2 * CONV_WIDTH:]
    zbuf[SUBLANES:SUBLANES + tm, :] = z
    cw = cw_ref[...]
    y = cw[CONV_TAPS - 1:CONV_TAPS, :] * z
    for tap in range(CONV_TAPS - 1):
        back = CONV_TAPS - 1 - tap
        y = y + cw[tap:tap + 1, :] * zbuf[SUBLANES - back:SUBLANES - back + tm, :]
    b_ref[...] = (u_b * y).astype(BF16)
    zbuf[0:SUBLANES, :] = zbuf[tm:tm + SUBLANES, :]

    f = jnp.dot(h, wf_ref[...], preferred_element_type=F32)
    f_ref[...] = jnp.transpose(f)[0:f_ref.shape[0], :]
    qkv = jnp.dot(h, wqkv_ref[...], preferred_element_type=F32)
    qkv_ref[:, :FOX_WIDTH] = (qkv[:, :FOX_WIDTH] * FOX_Q_SCALE).astype(BF16)
    qkv_ref[:, FOX_WIDTH:] = qkv[:, FOX_WIDTH:].astype(BF16)


def _ab_in(x2d, g, wqkv, wf, wconv, cw, *, seq, tm=1024):
    t = x2d.shape[0]
    kern = functools.partial(_ab_in_kernel, tm=tm, seq=seq)
    return pl.pallas_call(
        kern,
        grid=(t // tm,),
        in_specs=[
            pl.BlockSpec((tm, D_MODEL), lambda i: (i, 0)),
            _resident((1, D_MODEL)),
            _resident(wqkv.shape),
            _resident(wf.shape),
            _resident(wconv.shape),
            _resident(cw.shape),
        ],
        out_specs=[
            pl.BlockSpec((tm, 3 * FOX_WIDTH), lambda i: (i, 0)),
            pl.BlockSpec((FOX_HEADS, tm), lambda i: (0, i)),
            pl.BlockSpec((tm, CONV_WIDTH), lambda i: (i, 0)),
        ],
        out_shape=[
            jax.ShapeDtypeStruct((t, 3 * FOX_WIDTH), BF16),
            jax.ShapeDtypeStruct((FOX_HEADS, t), F32),
            jax.ShapeDtypeStruct((t, CONV_WIDTH), BF16),
        ],
        scratch_shapes=[pltpu.VMEM((tm + SUBLANES, CONV_WIDTH), F32)],
        compiler_params=_params("arbitrary"),
        name="ab_in",
    )(x2d, g, wqkv, wf, wconv, cw)


def _fox_cumsum_kernel(f_ref, bias_ref, c_ref, *, seq):
    heads, total = c_ref.shape
    x = f_ref[...] + bias_ref[...]
    ls = jnp.minimum(x, 0.0) - jnp.log(1.0 + jnp.exp(-jnp.abs(x)))
    lane = lax.broadcasted_iota(jnp.int32, (heads, LANES), 1)
    blocks = [ls[:, j * LANES:(j + 1) * LANES] for j in range(total // LANES)]
    shift = 1
    while shift < LANES:
        blocks = [blk + jnp.where(lane >= shift, pltpu.roll(blk, shift, 1), 0.0) for blk in blocks]
        shift *= 2
    per_seq = seq // LANES
    for j, blk in enumerate(blocks):
        if j % per_seq:
            blk = blk + carry
        c_ref[:, j * LANES:(j + 1) * LANES] = blk * LOG2E
        carry = blk[:, LANES - 1:LANES]


def _fox_cumsum(f_rows, bias_col, *, seq):
    heads, total = bias_col.shape[0], f_rows.shape[1]
    return pl.pallas_call(
        functools.partial(_fox_cumsum_kernel, seq=seq),
        grid=(1,),
        in_specs=[pl.BlockSpec((heads, total), lambda i: (0, 0)), _resident(bias_col.shape)],
        out_specs=pl.BlockSpec((heads, total), lambda i: (0, 0)),
        out_shape=jax.ShapeDtypeStruct((heads, total), F32),
        compiler_params=_params("arbitrary"),
        name="fox_cumsum",
    )(f_rows, bias_col)


def _fox_attn_kernel(q_ref, k_ref, v_ref, crow_ref, *rest, tq, cast_cols):
    cast_in, o_ref, cast_out = rest[:len(cast_cols)], rest[len(cast_cols)], rest[len(cast_cols) + 1:]
    _cast_slabs(cast_cols, cast_in, cast_out)
    nblk = k_ref.shape[1] // tq
    lane = lax.broadcasted_iota(jnp.int32, (1, LANES), 1)
    row = lax.broadcasted_iota(jnp.int32, (tq, tq), 0)
    col = lax.broadcasted_iota(jnp.int32, (tq, tq), 1)
    causal = row >= col
    in_head = [(lane >= hh * FOX_HEAD_DIM) & (lane < (hh + 1) * FOX_HEAD_DIM) for hh in range(2)]
    for pair in range(FOX_PAIRS_PER_STEP):
        cols = slice(pair * LANES, (pair + 1) * LANES)
        rows = lambda ref, c: ref[0, c * tq:(c + 1) * tq, cols]
        c_of = lambda hh: 2 * pair + hh

        probs = [(c, hh) for c in range(nblk) for hh in range(2)]
        s = []
        for c, hh in probs:
            q = rows(q_ref, c)
            qh = jnp.where(in_head[hh], q, jnp.zeros_like(q))
            sj = []
            for j in range(c + 1):
                sh = lax.dot_general(qh, rows(k_ref, j), NT_DIMS, preferred_element_type=F32)
                sh = sh - crow_ref[c_of(hh), 0, j:j + 1, :]
                sj.append(jnp.where(causal, sh, NEG_BIG) if j == c else sh)
            s.append(sj)
        shift = []
        for (c, hh), sj in zip(probs, s):
            mx = None
            for blk in sj:
                for cb in range(tq // LANES):
                    part = blk[:, cb * LANES:(cb + 1) * LANES]
                    mx = part if mx is None else jnp.maximum(mx, part)
            ct = jnp.transpose(
                jnp.broadcast_to(crow_ref[c_of(hh), 0, c:c + 1, :], (SUBLANES, tq)))[:, 0:1]
            m = jnp.max(mx, axis=-1, keepdims=True) + ct
            shift.append(m - ct)
        outs = []
        for (c, hh), sj, sh in zip(probs, s, shift):
            acc = None
            for j in range(c + 1):
                v = rows(v_ref, j)
                v_aug = jnp.concatenate([v, jnp.ones_like(v)], axis=1)
                d = jnp.dot(jnp.exp2(sj[j] - sh).astype(BF16), v_aug, preferred_element_type=F32)
                acc = d if acc is None else acc + d
            outs.append(acc[:, :LANES] / acc[:, LANES:])
        for c in range(nblk):
            o_ref[0, c * tq:(c + 1) * tq, cols] = jnp.where(
                lane < FOX_HEAD_DIM, outs[2 * c], outs[2 * c + 1]).astype(o_ref.dtype)


def _fox_attn(qkv, c_row, cast_jobs, *, tq):
    bsz, seq, _ = qkv.shape
    per = FOX_PAIRS_PER_STEP
    n_groups = FOX_WIDTH // (per * LANES)
    nblk = seq // tq
    seq_block = lambda col0: pl.BlockSpec((1, seq, per * LANES), lambda b, p: (b, 0, col0 + p))
    c_ins, c_in_specs, c_out_specs, c_out_shapes = _cast_operands(
        cast_jobs, bsz * n_groups, lambda b, p: b * n_groups + p)
    outs = pl.pallas_call(
        functools.partial(_fox_attn_kernel, tq=tq, cast_cols=[job[-1] for job in cast_jobs]),
        grid=(bsz, n_groups),
        in_specs=[
            seq_block(0),
            seq_block(n_groups),
            seq_block(2 * n_groups),
            pl.BlockSpec((2 * per, 1, nblk, tq), lambda b, p: (p, b, 0, 0)),
        ] + c_in_specs,
        out_specs=[seq_block(0)] + c_out_specs,
        out_shape=[jax.ShapeDtypeStruct((bsz, seq, FOX_WIDTH), BF16)] + c_out_shapes,
        compiler_params=_params("parallel", "parallel"),
        name="fox_attn",
    )(qkv, qkv, qkv, c_row, *c_ins)
    return outs[0], _cast_results(cast_jobs, outs[1:])


def _post_kernel(*refs, n_parts, final):
    parts = refs[:n_parts]
    x_ref, wo_ref, g_ref, w1_ref, w2_ref, gf_ref, o_ref = refs[n_parts:]
    hidden = w2_ref.shape[0]
    mix = None
    off = 0
    for p_ref in parts:
        width = p_ref.shape[1]
        d = jnp.dot(p_ref[...], wo_ref[off:off + width, :], preferred_element_type=F32)
        mix = d if mix is None else mix + d
        off += width
    x1 = x_ref[...] + mix
    h = _rms_norm(x1, g_ref[...]).astype(BF16)
    x2 = x1
    for c0 in range(0, hidden, FFN_COL_BLOCK):
        c1 = c0 + FFN_COL_BLOCK
        gate = jnp.dot(h, w1_ref[:, c0:c1], preferred_element_type=F32)
        up = jnp.dot(h, w1_ref[:, hidden + c0:hidden + c1], preferred_element_type=F32)
        act = (_silu(gate) * up).astype(BF16)
        x2 = x2 + jnp.dot(act, w2_ref[c0:c1, :], preferred_element_type=F32)
    if final:
        x2 = _rms_norm(x2, gf_ref[...])
    o_ref[...] = x2


def _post(parts, x2d, wo, g, w1, w2, gf, *, final, tm=1024):
    t = x2d.shape[0]
    assert w2.shape[0] % FFN_COL_BLOCK == 0
    kern = functools.partial(_post_kernel, n_parts=len(parts), final=final)
    in_specs = [pl.BlockSpec((tm, p.shape[1]), lambda i: (i, 0)) for p in parts]
    in_specs += [
        pl.BlockSpec((tm, D_MODEL), lambda i: (i, 0)),
        _resident(wo.shape),
        _resident((1, D_MODEL)),
        _resident(w1.shape),
        _resident(w2.shape),
        _resident((1, D_MODEL)),
    ]
    return pl.pallas_call(
        kern,
        grid=(t // tm,),
        in_specs=in_specs,
        out_specs=pl.BlockSpec((tm, D_MODEL), lambda i: (i, 0)),
        out_shape=jax.ShapeDtypeStruct((t, D_MODEL), F32),
        compiler_params=_params("parallel"),
        name="post",
    )(*parts, x2d, wo, g, w1, w2, gf)


def _hgrn_levels():
    h, out = HGRN_CHUNK // 2, []
    while h >= 1:
        out.append(h)
        h //= 2
    return out


def _hgrn_cum_table():
    t = np.arange(HGRN_CHUNK)
    m = (t[None, :] <= t[:, None]).astype(np.float32)
    return np.concatenate([m, m, m], axis=1)


def _hgrn_masks():
    c = HGRN_CHUNK
    levels = _hgrn_levels()
    row = lax.broadcasted_iota(jnp.int32, (c, HGRN_HEAD_DIM), 0)
    r2 = lax.broadcasted_iota(jnp.int32, (c, c), 0)
    c2 = lax.broadcasted_iota(jnp.int32, (c, c), 1)
    upper = [(row & h) != 0 for h in levels]
    pair = [((r2 // (2 * h)) == (c2 // (2 * h))) & ((r2 & h) != 0) & ((c2 & h) == 0)
            for h in levels]
    sign = [jnp.where(u, 1.0, -1.0) for u in upper]
    return row, upper, pair, sign, r2 == c2


def _hgrn_block(q16, f_logit, v, g16, lb, hn, cum, st, masks):
    c = HGRN_CHUNK
    levels = _hgrn_levels()
    row, upper, pair, sign, diag = masks

    n = q16.shape[0] // c
    chunk = lambda arr, i: arr[i * c:(i + 1) * c]
    q = q16.astype(F32)
    sig = pl.reciprocal(1.0 + jnp.exp2(f_logit * (-LOG2E)), approx=True)
    f = lb + (1.0 - lb) * sig
    k = 1.0 - f
    lg = jnp.log2(f)
    hi = lg.astype(BF16)
    r1 = lg - hi.astype(F32)
    mid = r1.astype(BF16)
    lo = (r1 - mid.astype(F32)).astype(BF16)
    b = [jnp.dot(cum, jnp.concatenate([chunk(hi, i), chunk(mid, i), chunk(lo, i)], axis=0),
                 preferred_element_type=F32) for i in range(n)]

    qk = jnp.sum(q * k, axis=-1, keepdims=True)

    a = [None] * n
    o_inter = []
    for li, h in enumerate(levels):
        xs = []
        for i in range(n):
            qi, ki, bi = chunk(q, i), chunk(k, i), b[i]
            if h == 1:
                x = jnp.where(upper[li], qi * chunk(f, i), ki)
            else:
                if h >= SUBLANES:
                    ref = jnp.concatenate(
                        [jnp.broadcast_to(bi[base + h - 1:base + h, :], (2 * h, HGRN_HEAD_DIM))
                         for base in range(0, c, 2 * h)], axis=0)
                else:
                    b3 = bi.reshape(c // SUBLANES, SUBLANES, HGRN_HEAD_DIM)
                    refs = [jnp.broadcast_to(b3[:, base + h - 1:base + h, :], b3.shape).reshape(bi.shape)
                            for base in range(0, SUBLANES, 2 * h)]
                    ref = refs[0]
                    for m in range(1, len(refs)):
                        ref = jnp.where((row % SUBLANES) >= m * 2 * h, refs[m], ref)
                x = jnp.where(upper[li], qi, ki) * jnp.exp2((bi - ref) * sign[li])
            xs.append(x.astype(BF16))
        ps = [lax.dot_general(x, x, NT_DIMS, preferred_element_type=F32) for x in xs]
        a = [jnp.where(pair[li], ps[i], 0.0 if a[i] is None else a[i]) for i in range(n)]

        for i in range(li * n // len(levels), (li + 1) * n // len(levels)):
            b_last = b[i][c - 1:c, :]
            kv = lax.dot_general(chunk(v, i), (chunk(k, i) * jnp.exp2(b_last - b[i])).astype(BF16),
                                 TN_DIMS, preferred_element_type=F32)
            o_inter.append(lax.dot_general((chunk(q, i) * jnp.exp2(b[i])).astype(BF16),
                                           st.astype(BF16), NT_DIMS, preferred_element_type=F32))
            st = st * jnp.exp2(b_last) + kv
    a = [jnp.where(diag, chunk(qk, i), a[i]) for i in range(n)]
    o_intra = [jnp.dot(a[i].astype(BF16), chunk(v, i), preferred_element_type=F32) for i in range(n)]

    o = jnp.concatenate([o_inter[i] + o_intra[i] for i in range(n)], axis=0)
    o = o * lax.rsqrt(jnp.mean(o * o, axis=-1, keepdims=True) + EPS) * hn
    g = g16.astype(F32)
    return (o * _silu(g)).astype(BF16), st


def _c_in_kernel(x_ref, g_ref, w_ref, o_ref, f_ref):
    h = _rms_norm(x_ref[...], g_ref[...]).astype(BF16)
    pr = jnp.dot(h, w_ref[...], preferred_element_type=F32)
    o_ref[:, :D_MODEL] = pr[:, :D_MODEL].astype(BF16)
    o_ref[:, D_MODEL:] = pr[:, 2 * D_MODEL:].astype(BF16)
    f_ref[...] = pr[:, D_MODEL:2 * D_MODEL]


def _c_in(x2d, g, w, *, tm=1024):
    t = x2d.shape[0]
    return pl.pallas_call(
        _c_in_kernel,
        grid=(t // tm,),
        in_specs=[
            pl.BlockSpec((tm, D_MODEL), lambda i: (i, 0)),
            _resident((1, D_MODEL)),
            _resident(w.shape),
        ],
        out_specs=[
            pl.BlockSpec((tm, 3 * D_MODEL), lambda i: (i, 0)),
            pl.BlockSpec((tm, D_MODEL), lambda i: (i, 0)),
        ],
        out_shape=[
            jax.ShapeDtypeStruct((t, 3 * D_MODEL), BF16),
            jax.ShapeDtypeStruct((t, D_MODEL), F32),
        ],
        compiler_params=_params("parallel"),
        name="c_in",
    )(x2d, g, w)


HGRN_HEADS_PER_STEP = 4


def _hgrn_kernel(q_ref, f_ref, i_ref, g_ref, lbraw_ref, hn_ref, cum_ref, *rest,
                 layer_idx, cast_cols):
    n_cast = len(cast_cols)
    cast_in, o_ref, cast_out, state = rest[:n_cast], rest[n_cast], rest[n_cast + 1:-1], rest[-1]
    _cast_slabs(cast_cols, cast_in, cast_out)

    @pl.when(pl.program_id(2) == 0)
    def _():
        state[...] = jnp.zeros_like(state)

    raw = lbraw_ref[...]
    e = jnp.exp(raw - jnp.max(raw, axis=0, keepdims=True))
    sm = e / jnp.sum(e, axis=0, keepdims=True)
    lb = jnp.sum(sm[0:layer_idx + 1, :], axis=0, keepdims=True) - sm[0:1, :]
    hn = hn_ref[...]
    masks = _hgrn_masks()
    for n in range(HGRN_HEADS_PER_STEP):
        hs = slice(n * HGRN_HEAD_DIM, (n + 1) * HGRN_HEAD_DIM)
        o, st = _hgrn_block(q_ref[0, :, hs], f_ref[0, :, hs], i_ref[0, :, hs], g_ref[0, :, hs],
                            lb[:, hs], hn[:, hs], cum_ref[...], state[n], masks)
        state[n] = st
        o_ref[0, :, hs] = o


def _hgrn(proj, f_logit, lb_raw, head_norm, cast_jobs, *, layer_idx, blk=2048):
    bsz, seq, _ = f_logit.shape
    hd = HGRN_HEAD_DIM
    per = HGRN_HEADS_PER_STEP
    width = per * hd
    ng = HGRN_HEADS // per
    n_l = seq // blk
    cum = jnp.asarray(_hgrn_cum_table(), BF16)
    c_ins, c_in_specs, c_out_specs, c_out_shapes = _cast_operands(
        cast_jobs, bsz * ng * n_l, lambda b, h, l: (b * ng + h) * n_l + l)
    outs = pl.pallas_call(
        functools.partial(_hgrn_kernel, layer_idx=layer_idx,
                          cast_cols=[job[-1] for job in cast_jobs]),
        grid=(bsz, ng, n_l),
        in_specs=[
            pl.BlockSpec((1, blk, width), lambda b, h, l: (b, l, h)),
            pl.BlockSpec((1, blk, width), lambda b, h, l: (b, l, h)),
            pl.BlockSpec((1, blk, width), lambda b, h, l: (b, l, ng + h)),
            pl.BlockSpec((1, blk, width), lambda b, h, l: (b, l, 2 * ng + h)),
            pl.BlockSpec((lb_raw.shape[0], width), lambda b, h, l: (0, h)),
            pl.BlockSpec((1, width), lambda b, h, l: (0, h)),
            _resident(cum.shape),
        ] + c_in_specs,
        out_specs=[pl.BlockSpec((1, blk, width), lambda b, h, l: (b, l, h))] + c_out_specs,
        out_shape=[jax.ShapeDtypeStruct((bsz, seq, D_MODEL), BF16)] + c_out_shapes,
        scratch_shapes=[pltpu.VMEM((per, hd, hd), F32)],
        compiler_params=_params("parallel", "parallel", "arbitrary"),
        name="hgrn",
    )(proj, f_logit, proj, proj, lb_raw, head_norm, cum, *c_ins)
    return outs[0], _cast_results(cast_jobs, outs[1:])


def kernel(x, norm_mix, norm_ffn, final_norm, ab_w_in, fox_f_bias, conv_w, ab_w_out,
           c_w_in, c_lower_bounds, c_head_norm, c_w_out, ffn_w_in, ffn_w_out):
    bsz, seq, d = x.shape
    depth = norm_mix.shape[0]
    t = bsz * seq
    x2d = x.reshape(t, d)
    gf = final_norm.reshape(1, d)
    q_end = 3 * FOX_WIDTH
    f_end = q_end + FOX_HEADS

    ab_cols = [(0, q_end), (q_end, q_end + LANES), (f_end, f_end + 3 * CONV_WIDTH)]
    whole = lambda w, layer: (w, layer, [(0, w.shape[2])])

    def in_proj_job(layer):
        j = layer // 2
        return (ab_w_in, j, ab_cols) if layer % 2 == 0 else whole(c_w_in, j)

    w0, j0, cols0 = in_proj_job(0)
    in_w = [w0[j0, :, c0:c1].astype(BF16) for c0, c1 in cols0]

    for layer in range(depth):
        j = layer // 2
        g_mix = norm_mix[layer].reshape(1, d)
        jobs = [whole(ab_w_out if layer % 2 == 0 else c_w_out, j),
                whole(ffn_w_in, layer), whole(ffn_w_out, layer)]
        if layer + 1 < depth:
            jobs.append(in_proj_job(layer + 1))
        if layer % 2 == 0:
            wqkv, wf, wconv = in_w
            qkv, f_rows, b_out = _ab_in(x2d, g_mix, wqkv, wf, wconv, conv_w[j], seq=seq)
            c = _fox_cumsum(f_rows, fox_f_bias[j].reshape(FOX_HEADS, 1), seq=seq)
            tq = FOX_Q_TILE
            c_row = c.reshape(FOX_HEADS, bsz, seq // tq, tq)
            a_out, cast = _fox_attn(qkv.reshape(bsz, seq, q_end), c_row, jobs, tq=tq)
            parts = [a_out.reshape(t, FOX_WIDTH), b_out]
        else:
            proj, f_logit = _c_in(x2d, g_mix, in_w[0])
            o, cast = _hgrn(proj.reshape(bsz, seq, 3 * d), f_logit.reshape(bsz, seq, d),
                            c_lower_bounds, c_head_norm[j].reshape(1, d), jobs, layer_idx=j)
            parts = [o.reshape(t, d)]
        (wo,), (w1,), (w2,) = cast[:3]
        in_w = cast[3] if layer + 1 < depth else None
        x2d = _post(parts, x2d, wo, norm_ffn[layer].reshape(1, d), w1, w2, gf,
                    final=(layer == depth - 1))
    return x2d.reshape(bsz, seq, d)
```

```python
import functools

import jax
import jax.numpy as jnp
import numpy as np
from jax import lax
from jax.experimental import pallas as pl
from jax.experimental.pallas import tpu as pltpu

D_MODEL = 1024
EPS = 1e-6
NEG_BIG = -1e30

FOX_HEADS = 8
FOX_HEAD_DIM = 64
FOX_WIDTH = FOX_HEADS * FOX_HEAD_DIM
CONV_WIDTH = D_MODEL - FOX_WIDTH
CONV_TAPS = 3
LOG2E = 1.4426950408889634
FOX_Q_SCALE = LOG2E * FOX_HEAD_DIM ** -0.5
FOX_Q_TILE = 256
FFN_COL_BLOCK = 256
HGRN_HEADS = 8
HGRN_HEAD_DIM = D_MODEL // HGRN_HEADS
HGRN_CHUNK = 64

LANES = 128
SUBLANES = 8
VMEM_LIMIT = 56 * 1024 * 1024

BF16 = jnp.bfloat16
F32 = jnp.float32

NT_DIMS = (((1,), (1,)), ((), ()))
TN_DIMS = (((0,), (0,)), ((), ()))


def _resident(shape):
    zeros = (0,) * len(shape)
    return pl.BlockSpec(shape, lambda *_: zeros, pipeline_mode=pl.Buffered(1))


def _params(*sem):
    return pltpu.CompilerParams(dimension_semantics=sem, vmem_limit_bytes=VMEM_LIMIT)


def _rms_norm(x, g):
    return x * lax.rsqrt(jnp.mean(x * x, axis=-1, keepdims=True) + EPS) * g


def _silu(x):
    half = 0.5 * x
    return half * (jnp.tanh(half) + 1.0)


BF16_SUBLANES = 2 * SUBLANES


def _cast_slabs_per_job(rows, n_steps):
    return max(k for k in range(1, n_steps + 1)
               if rows % k == 0 and (rows // k) % BF16_SUBLANES == 0)


def _cast_operands(jobs, n_steps, step_of):
    ins, in_specs, out_specs, out_shapes = [], [], [], []
    for w, layer, transposed, col_ranges in jobs:
        if transposed:
            rows = w.shape[2]
            k = rows // LANES
            assert k <= n_steps
            last = lambda *g, k=k: jnp.minimum(step_of(*g), k - 1)
            ins.append(w)
            in_specs.append(pl.BlockSpec((1, w.shape[1], LANES),
                                         lambda *g, layer=layer, last=last: (layer, 0, last(*g))))
            for c0, c1 in col_ranges:
                out_specs.append(pl.BlockSpec((LANES, c1 - c0), lambda *g, last=last: (last(*g), 0)))
                out_shapes.append(jax.ShapeDtypeStruct((rows, c1 - c0), BF16))
            continue
        layers, rows, cols = w.shape
        k = _cast_slabs_per_job(rows, n_steps)
        slab = rows // k
        in_map = lambda *g, k=k, layer=layer: (layer * k + jnp.minimum(step_of(*g), k - 1), 0, 0)
        out_map = lambda *g, k=k: (jnp.minimum(step_of(*g), k - 1), 0, 0)
        ins.append(w.reshape(layers * k, slab, cols))
        in_specs.append(pl.BlockSpec((1, slab, cols), in_map))
        for c0, c1 in col_ranges:
            out_specs.append(pl.BlockSpec((1, slab, c1 - c0), out_map))
            out_shapes.append(jax.ShapeDtypeStruct((k, slab, c1 - c0), BF16))
    return ins, in_specs, out_specs, out_shapes


def _cast_slabs(jobs_cols, in_refs, out_refs):
    outs = iter(out_refs)
    for ref, (transposed, col_ranges) in zip(in_refs, jobs_cols):
        for c0, c1 in col_ranges:
            if transposed:
                next(outs)[...] = jnp.transpose(ref[0, c0:c1, :]).astype(BF16)
            else:
                next(outs)[0] = ref[0, :, c0:c1].astype(BF16)


def _cast_results(jobs, outs):
    outs = iter(outs)
    return [[next(outs).reshape(w.shape[2 if transposed else 1], c1 - c0) for c0, c1 in col_ranges]
            for w, _, transposed, col_ranges in jobs]


def _ab_in_kernel(x_ref, g_ref, wqkv_ref, wf_ref, wconv_ref, cw_ref,
                  qkv_ref, f_ref, b_ref, zbuf, *, tm, seq):
    i = pl.program_id(0)

    @pl.when((i * tm) % seq == 0)
    def _():
        zbuf[0:SUBLANES, :] = jnp.zeros((SUBLANES, CONV_WIDTH), F32)

    h = _rms_norm(x_ref[...], g_ref[...]).astype(BF16)
    u = jnp.dot(h, wconv_ref[...], preferred_element_type=F32)
    u_b = u[:, :CONV_WIDTH]
    z = u[:, CONV_WIDTH:2 * CONV_WIDTH] * u[:, 2 * CONV_WIDTH:]
    zbuf[SUBLANES:SUBLANES + tm, :] = z
    cw = cw_ref[...]
    y = cw[CONV_TAPS - 1:CONV_TAPS, :] * z
    for tap in range(CONV_TAPS - 1):
        back = CONV_TAPS - 1 - tap
        y = y + cw[tap:tap + 1, :] * zbuf[SUBLANES - back:SUBLANES - back + tm, :]
    b_ref[...] = (u_b * y).astype(BF16)
    zbuf[0:SUBLANES, :] = zbuf[tm:tm + SUBLANES, :]

    f = jnp.dot(h, wf_ref[...], preferred_element_type=F32)
    f_ref[...] = jnp.transpose(f)[0:f_ref.shape[0], :]
    qkv = jnp.dot(h, wqkv_ref[...], preferred_element_type=F32)
    qkv_ref[:, :FOX_WIDTH] = (qkv[:, :FOX_WIDTH] * FOX_Q_SCALE).astype(BF16)
    qkv_ref[:, FOX_WIDTH:] = qkv[:, FOX_WIDTH:].astype(BF16)


def _ab_in(x2d, g, wqkv, wf, wconv, cw, *, seq, tm=1024):
    t = x2d.shape[0]
    kern = functools.partial(_ab_in_kernel, tm=tm, seq=seq)
    return pl.pallas_call(
        kern,
        grid=(t // tm,),
        in_specs=[
            pl.BlockSpec((tm, D_MODEL), lambda i: (i, 0)),
            _resident((1, D_MODEL)),
            _resident(wqkv.shape),
            _resident(wf.shape),
            _resident(wconv.shape),
            _resident(cw.shape),
        ],
        out_specs=[
            pl.BlockSpec((tm, 3 * FOX_WIDTH), lambda i: (i, 0)),
            pl.BlockSpec((FOX_HEADS, tm), lambda i: (0, i)),
            pl.BlockSpec((tm, CONV_WIDTH), lambda i: (i, 0)),
        ],
        out_shape=[
            jax.ShapeDtypeStruct((t, 3 * FOX_WIDTH), BF16),
            jax.ShapeDtypeStruct((FOX_HEADS, t), F32),
            jax.ShapeDtypeStruct((t, CONV_WIDTH), BF16),
        ],
        scratch_shapes=[pltpu.VMEM((tm + SUBLANES, CONV_WIDTH), F32)],
        compiler_params=_params("arbitrary"),
        name="ab_in",
    )(x2d, g, wqkv, wf, wconv, cw)


def _fox_cumsum_kernel(f_ref, bias_ref, c_ref, *, seq):
    heads, total = c_ref.shape
    x = f_ref[...] + bias_ref[...]
    ls = jnp.minimum(x, 0.0) - jnp.log(1.0 + jnp.exp(-jnp.abs(x)))
    lane = lax.broadcasted_iota(jnp.int32, (heads, LANES), 1)
    blocks = [ls[:, j * LANES:(j + 1) * LANES] for j in range(total // LANES)]
    shift = 1
    while shift < LANES:
        blocks = [blk + jnp.where(lane >= shift, pltpu.roll(blk, shift, 1), 0.0) for blk in blocks]
        shift *= 2
    per_seq = seq // LANES
    for j, blk in enumerate(blocks):
        if j % per_seq:
            blk = blk + carry
        c_ref[:, j * LANES:(j + 1) * LANES] = blk * LOG2E
        carry = blk[:, LANES - 1:LANES]


def _fox_cumsum(f_rows, bias_col, *, seq):
    heads, total = bias_col.shape[0], f_rows.shape[1]
    return pl.pallas_call(
        functools.partial(_fox_cumsum_kernel, seq=seq),
        grid=(1,),
        in_specs=[pl.BlockSpec((heads, total), lambda i: (0, 0)), _resident(bias_col.shape)],
        out_specs=pl.BlockSpec((heads, total), lambda i: (0, 0)),
        out_shape=jax.ShapeDtypeStruct((heads, total), F32),
        compiler_params=_params("arbitrary"),
        name="fox_cumsum",
    )(f_rows, bias_col)


def _fox_attn_kernel(q_ref, k_ref, v_ref, crow_ref, *rest, tq, cast_cols):
    cast_in, o_ref, cast_out = rest[:len(cast_cols)], rest[len(cast_cols)], rest[len(cast_cols) + 1:]
    _cast_slabs(cast_cols, cast_in, cast_out)
    nblk = k_ref.shape[1] // tq
    lane = lax.broadcasted_iota(jnp.int32, (1, LANES), 1)
    row = lax.broadcasted_iota(jnp.int32, (tq, tq), 0)
    col = lax.broadcasted_iota(jnp.int32, (tq, tq), 1)
    causal = row >= col
    in_head = [(lane >= hh * FOX_HEAD_DIM) & (lane < (hh + 1) * FOX_HEAD_DIM) for hh in range(2)]
    rows = lambda ref, c: ref[0, c * tq:(c + 1) * tq, :]

    probs = [(c, hh) for c in range(nblk) for hh in range(2)]
    s = []
    for c, hh in probs:
        q = rows(q_ref, c)
        qh = jnp.where(in_head[hh], q, jnp.zeros_like(q))
        sj = []
        for j in range(c + 1):
            sh = lax.dot_general(qh, rows(k_ref, j), NT_DIMS, preferred_element_type=F32)
            sh = sh - crow_ref[hh, 0, j:j + 1, :]
            sj.append(jnp.where(causal, sh, NEG_BIG) if j == c else sh)
        s.append(sj)
    shift = []
    for (c, hh), sj in zip(probs, s):
        mx = None
        for blk in sj:
            for cb in range(tq // LANES):
                part = blk[:, cb * LANES:(cb + 1) * LANES]
                mx = part if mx is None else jnp.maximum(mx, part)
        ct = jnp.transpose(jnp.broadcast_to(crow_ref[hh, 0, c:c + 1, :], (SUBLANES, tq)))[:, 0:1]
        m = jnp.max(mx, axis=-1, keepdims=True) + ct
        shift.append(m - ct)
    outs = []
    for (c, hh), sj, sh in zip(probs, s, shift):
        acc = None
        for j in range(c + 1):
            v = rows(v_ref, j)
            v_aug = jnp.concatenate([v, jnp.ones_like(v)], axis=1)
            d = jnp.dot(jnp.exp2(sj[j] - sh).astype(BF16), v_aug, preferred_element_type=F32)
            acc = d if acc is None else acc + d
        outs.append(acc[:, :LANES] / acc[:, LANES:])
    for c in range(nblk):
        o_ref[0, c * tq:(c + 1) * tq, :] = jnp.where(
            lane < FOX_HEAD_DIM, outs[2 * c], outs[2 * c + 1]).astype(o_ref.dtype)


def _fox_attn(qkv, c_row, cast_jobs, *, tq):
    bsz, seq, _ = qkv.shape
    n_pairs = FOX_WIDTH // LANES
    nblk = seq // tq
    seq_block = lambda col0: pl.BlockSpec((1, seq, LANES), lambda b, p: (b, 0, col0 + p))
    c_ins, c_in_specs, c_out_specs, c_out_shapes = _cast_operands(
        cast_jobs, bsz * n_pairs, lambda b, p: b * n_pairs + p)
    outs = pl.pallas_call(
        functools.partial(_fox_attn_kernel, tq=tq, cast_cols=[job[2:] for job in cast_jobs]),
        grid=(bsz, n_pairs),
        in_specs=[
            seq_block(0),
            seq_block(n_pairs),
            seq_block(2 * n_pairs),
            pl.BlockSpec((2, 1, nblk, tq), lambda b, p: (p, b, 0, 0)),
        ] + c_in_specs,
        out_specs=[seq_block(0)] + c_out_specs,
        out_shape=[jax.ShapeDtypeStruct((bsz, seq, FOX_WIDTH), BF16)] + c_out_shapes,
        compiler_params=_params("parallel", "parallel"),
        name="fox_attn",
    )(qkv, qkv, qkv, c_row, *c_ins)
    return outs[0], _cast_results(cast_jobs, outs[1:])


def _post_kernel(*refs, n_parts, final):
    parts = refs[:n_parts]
    x_ref, wo_ref, g_ref, w1_ref, w2_ref, gf_ref, o_ref = refs[n_parts:]
    hidden = w2_ref.shape[0]
    mix = None
    off = 0
    for p_ref in parts:
        width = p_ref.shape[1]
        d = jnp.dot(p_ref[...], wo_ref[off:off + width, :], preferred_element_type=F32)
        mix = d if mix is None else mix + d
        off += width
    x1 = x_ref[...] + mix
    h = _rms_norm(x1, g_ref[...]).astype(BF16)
    x2 = x1
    for c0 in range(0, hidden, FFN_COL_BLOCK):
        c1 = c0 + FFN_COL_BLOCK
        gate = jnp.dot(h, w1_ref[:, c0:c1], preferred_element_type=F32)
        up = jnp.dot(h, w1_ref[:, hidden + c0:hidden + c1], preferred_element_type=F32)
        act = (_silu(gate) * up).astype(BF16)
        x2 = x2 + jnp.dot(act, w2_ref[c0:c1, :], preferred_element_type=F32)
    if final:
        x2 = _rms_norm(x2, gf_ref[...])
    o_ref[...] = x2


def _post(parts, x2d, wo, g, w1, w2, gf, *, final, tm=1024):
    t = x2d.shape[0]
    assert w2.shape[0] % FFN_COL_BLOCK == 0
    kern = functools.partial(_post_kernel, n_parts=len(parts), final=final)
    in_specs = [pl.BlockSpec((tm, p.shape[1]), lambda i: (i, 0)) for p in parts]
    in_specs += [
        pl.BlockSpec((tm, D_MODEL), lambda i: (i, 0)),
        _resident(wo.shape),
        _resident((1, D_MODEL)),
        _resident(w1.shape),
        _resident(w2.shape),
        _resident((1, D_MODEL)),
    ]
    return pl.pallas_call(
        kern,
        grid=(t // tm,),
        in_specs=in_specs,
        out_specs=pl.BlockSpec((tm, D_MODEL), lambda i: (i, 0)),
        out_shape=jax.ShapeDtypeStruct((t, D_MODEL), F32),
        compiler_params=_params("parallel"),
        name="post",
    )(*parts, x2d, wo, g, w1, w2, gf)


def _hgrn_levels():
    h, out = HGRN_CHUNK // 2, []
    while h >= 1:
        out.append(h)
        h //= 2
    return out


def _hgrn_cum_table():
    t = np.arange(HGRN_CHUNK)
    m = (t[None, :] <= t[:, None]).astype(np.float32)
    return np.concatenate([m, m, m], axis=1)


def _hgrn_masks():
    c = HGRN_CHUNK
    levels = _hgrn_levels()
    row = lax.broadcasted_iota(jnp.int32, (c, HGRN_HEAD_DIM), 0)
    r2 = lax.broadcasted_iota(jnp.int32, (c, c), 0)
    c2 = lax.broadcasted_iota(jnp.int32, (c, c), 1)
    upper = [(row & h) != 0 for h in levels]
    pair = [((r2 // (2 * h)) == (c2 // (2 * h))) & ((r2 & h) != 0) & ((c2 & h) == 0)
            for h in levels]
    sign = [jnp.where(u, 1.0, -1.0) for u in upper]
    return row, upper, pair, sign, r2 == c2


def _hgrn_block(q16, f_logit, v, g16, lb, hn, cum, st, masks):
    c = HGRN_CHUNK
    levels = _hgrn_levels()
    row, upper, pair, sign, diag = masks

    n = q16.shape[0] // c
    chunk = lambda arr, i: arr[i * c:(i + 1) * c]
    q = q16.astype(F32)
    sig = pl.reciprocal(1.0 + jnp.exp2(f_logit * (-LOG2E)), approx=True)
    f = lb + (1.0 - lb) * sig
    k = 1.0 - f
    lg = jnp.log2(f)
    hi = lg.astype(BF16)
    r1 = lg - hi.astype(F32)
    mid = r1.astype(BF16)
    lo = (r1 - mid.astype(F32)).astype(BF16)
    b = [jnp.dot(cum, jnp.concatenate([chunk(hi, i), chunk(mid, i), chunk(lo, i)], axis=0),
                 preferred_element_type=F32) for i in range(n)]

    qk = jnp.sum(q * k, axis=-1, keepdims=True)

    a = [None] * n
    o_inter = []
    for li, h in enumerate(levels):
        xs = []
        for i in range(n):
            qi, ki, bi = chunk(q, i), chunk(k, i), b[i]
            if h == 1:
                x = jnp.where(upper[li], qi * chunk(f, i), ki)
            else:
                if h >= SUBLANES:
                    ref = jnp.concatenate(
                        [jnp.broadcast_to(bi[base + h - 1:base + h, :], (2 * h, HGRN_HEAD_DIM))
                         for base in range(0, c, 2 * h)], axis=0)
                else:
                    b3 = bi.reshape(c // SUBLANES, SUBLANES, HGRN_HEAD_DIM)
                    refs = [jnp.broadcast_to(b3[:, base + h - 1:base + h, :], b3.shape).reshape(bi.shape)
                            for base in range(0, SUBLANES, 2 * h)]
                    ref = refs[0]
                    for m in range(1, len(refs)):
                        ref = jnp.where((row % SUBLANES) >= m * 2 * h, refs[m], ref)
                x = jnp.where(upper[li], qi, ki) * jnp.exp2((bi - ref) * sign[li])
            xs.append(x.astype(BF16))
        ps = [lax.dot_general(x, x, NT_DIMS, preferred_element_type=F32) for x in xs]
        a = [jnp.where(pair[li], ps[i], 0.0 if a[i] is None else a[i]) for i in range(n)]

        for i in range(li * n // len(levels), (li + 1) * n // len(levels)):
            b_last = b[i][c - 1:c, :]
            kv = lax.dot_general(chunk(v, i), (chunk(k, i) * jnp.exp2(b_last - b[i])).astype(BF16),
                                 TN_DIMS, preferred_element_type=F32)
            o_inter.append(lax.dot_general((chunk(q, i) * jnp.exp2(b[i])).astype(BF16),
                                           st.astype(BF16), NT_DIMS, preferred_element_type=F32))
            st = st * jnp.exp2(b_last) + kv
    a = [jnp.where(diag, chunk(qk, i), a[i]) for i in range(n)]
    o_intra = [jnp.dot(a[i].astype(BF16), chunk(v, i), preferred_element_type=F32) for i in range(n)]

    o = jnp.concatenate([o_inter[i] + o_intra[i] for i in range(n)], axis=0)
    o = o * lax.rsqrt(jnp.mean(o * o, axis=-1, keepdims=True) + EPS) * hn
    g = g16.astype(F32)
    return (o * _silu(g)).astype(BF16), st


def _c_in_kernel(x_ref, g_ref, w_ref, o_ref, f_ref):
    h = _rms_norm(x_ref[...], g_ref[...]).astype(BF16)
    pr = jnp.dot(h, w_ref[...], preferred_element_type=F32)
    o_ref[:, :D_MODEL] = pr[:, :D_MODEL].astype(BF16)
    o_ref[:, D_MODEL:] = pr[:, 2 * D_MODEL:].astype(BF16)
    f_ref[...] = pr[:, D_MODEL:2 * D_MODEL]


def _c_in(x2d, g, w, *, tm=1024):
    t = x2d.shape[0]
    return pl.pallas_call(
        _c_in_kernel,
        grid=(t // tm,),
        in_specs=[
            pl.BlockSpec((tm, D_MODEL), lambda i: (i, 0)),
            _resident((1, D_MODEL)),
            _resident(w.shape),
        ],
        out_specs=[
            pl.BlockSpec((tm, 3 * D_MODEL), lambda i: (i, 0)),
            pl.BlockSpec((tm, D_MODEL), lambda i: (i, 0)),
        ],
        out_shape=[
            jax.ShapeDtypeStruct((t, 3 * D_MODEL), BF16),
            jax.ShapeDtypeStruct((t, D_MODEL), F32),
        ],
        compiler_params=_params("parallel"),
        name="c_in",
    )(x2d, g, w)


HGRN_HEADS_PER_STEP = 2


def _hgrn_kernel(q_ref, f_ref, i_ref, g_ref, lbraw_ref, hn_ref, cum_ref, *rest,
                 layer_idx, cast_cols):
    n_cast = len(cast_cols)
    cast_in, o_ref, cast_out, state = rest[:n_cast], rest[n_cast], rest[n_cast + 1:-1], rest[-1]
    _cast_slabs(cast_cols, cast_in, cast_out)

    @pl.when(pl.program_id(2) == 0)
    def _():
        state[...] = jnp.zeros_like(state)

    raw = lbraw_ref[...]
    e = jnp.exp(raw - jnp.max(raw, axis=0, keepdims=True))
    sm = e / jnp.sum(e, axis=0, keepdims=True)
    lb = jnp.sum(sm[0:layer_idx + 1, :], axis=0, keepdims=True) - sm[0:1, :]
    hn = hn_ref[...]
    masks = _hgrn_masks()
    for n in range(HGRN_HEADS_PER_STEP):
        hs = slice(n * HGRN_HEAD_DIM, (n + 1) * HGRN_HEAD_DIM)
        o, st = _hgrn_block(q_ref[0, :, hs], f_ref[0, :, hs], i_ref[0, :, hs], g_ref[0, :, hs],
                            lb[:, hs], hn[:, hs], cum_ref[...], state[n], masks)
        state[n] = st
        o_ref[0, :, hs] = o


def _hgrn(proj, f_logit, lb_raw, head_norm, cast_jobs, *, layer_idx, blk=2048):
    bsz, seq, _ = f_logit.shape
    hd = HGRN_HEAD_DIM
    per = HGRN_HEADS_PER_STEP
    width = per * hd
    ng = HGRN_HEADS // per
    n_l = seq // blk
    cum = jnp.asarray(_hgrn_cum_table(), BF16)
    c_ins, c_in_specs, c_out_specs, c_out_shapes = _cast_operands(
        cast_jobs, bsz * ng * n_l, lambda b, h, l: (b * ng + h) * n_l + l)
    outs = pl.pallas_call(
        functools.partial(_hgrn_kernel, layer_idx=layer_idx,
                          cast_cols=[job[2:] for job in cast_jobs]),
        grid=(bsz, ng, n_l),
        in_specs=[
            pl.BlockSpec((1, blk, width), lambda b, h, l: (b, l, h)),
            pl.BlockSpec((1, blk, width), lambda b, h, l: (b, l, h)),
            pl.BlockSpec((1, blk, width), lambda b, h, l: (b, l, ng + h)),
            pl.BlockSpec((1, blk, width), lambda b, h, l: (b, l, 2 * ng + h)),
            pl.BlockSpec((lb_raw.shape[0], width), lambda b, h, l: (0, h)),
            pl.BlockSpec((1, width), lambda b, h, l: (0, h)),
            _resident(cum.shape),
        ] + c_in_specs,
        out_specs=[pl.BlockSpec((1, blk, width), lambda b, h, l: (b, l, h))] + c_out_specs,
        out_shape=[jax.ShapeDtypeStruct((bsz, seq, D_MODEL), BF16)] + c_out_shapes,
        scratch_shapes=[pltpu.VMEM((per, hd, hd), F32)],
        compiler_params=_params("parallel", "parallel", "arbitrary"),
        name="hgrn",
    )(proj, f_logit, proj, proj, lb_raw, head_norm, cum, *c_ins)
    return outs[0], _cast_results(cast_jobs, outs[1:])


def kernel(x, norm_mix, norm_ffn, final_norm, ab_w_in, fox_f_bias, conv_w, ab_w_out,
           c_w_in, c_lower_bounds, c_head_norm, c_w_out, ffn_w_in, ffn_w_out):
    bsz, seq, d = x.shape
    depth = norm_mix.shape[0]
    t = bsz * seq
    x2d = x.reshape(t, d)
    gf = final_norm.reshape(1, d)
    q_end = 3 * FOX_WIDTH
    f_end = q_end + FOX_HEADS

    ab_cols = [(0, q_end), (q_end, q_end + LANES), (f_end, f_end + 3 * CONV_WIDTH)]
    whole = lambda w, layer: (w, layer, False, [(0, w.shape[2])])
    ab_w_in_t = jnp.swapaxes(ab_w_in, 1, 2)

    def in_proj_job(layer):
        j = layer // 2
        return (ab_w_in_t, j, True, ab_cols) if layer % 2 == 0 else whole(c_w_in, j)

    in_w = [ab_w_in_t[0, c0:c1, :].T.astype(BF16) for c0, c1 in ab_cols]

    for layer in range(depth):
        j = layer // 2
        g_mix = norm_mix[layer].reshape(1, d)
        jobs = [whole(ab_w_out if layer % 2 == 0 else c_w_out, j),
                whole(ffn_w_in, layer), whole(ffn_w_out, layer)]
        if layer + 1 < depth:
            jobs.append(in_proj_job(layer + 1))
        if layer % 2 == 0:
            wqkv, wf, wconv = in_w
            qkv, f_rows, b_out = _ab_in(x2d, g_mix, wqkv, wf, wconv, conv_w[j], seq=seq)
            c = _fox_cumsum(f_rows, fox_f_bias[j].reshape(FOX_HEADS, 1), seq=seq)
            tq = FOX_Q_TILE
            c_row = c.reshape(FOX_HEADS, bsz, seq // tq, tq)
            a_out, cast = _fox_attn(qkv.reshape(bsz, seq, q_end), c_row, jobs, tq=tq)
            parts = [a_out.reshape(t, FOX_WIDTH), b_out]
        else:
            proj, f_logit = _c_in(x2d, g_mix, in_w[0])
            o, cast = _hgrn(proj.reshape(bsz, seq, 3 * d), f_logit.reshape(bsz, seq, d),
                            c_lower_bounds, c_head_norm[j].reshape(1, d), jobs, layer_idx=j)
            parts = [o.reshape(t, d)]
        (wo,), (w1,), (w2,) = cast[:3]
        in_w = cast[3] if layer + 1 < depth else None
        x2d = _post(parts, x2d, wo, norm_ffn[layer].reshape(1, d), w1, w2, gf,
                    final=(layer == depth - 1))
    return x2d.reshape(bsz, seq, d)
```
